```python
import jax, jax.numpy as jnp
from jax import lax
import numpy as np

D_MODEL = 1024
BATCH = 2
SEQ = 8192
DEPTH = 2

CHUNK = 64
CONV_W = 4
BR_W = 512
N_BRANCH = 3
DN_HEADS = 4
DN_DK = 128
DN_DV = 128
LRU_W = BR_W
LRU_BLOCKS = 8
LRU_BLK = LRU_W // LRU_BLOCKS
LRU_C = 8.0
LA_HEADS = 8
LA_HD = 64
LA_PAST = 8
LA_BAND = (LA_PAST + 1) * CHUNK
REL_CLIP = 128
MEM_LEN = 256
XA_HEADS = 4
XA_HD = D_MODEL // XA_HEADS
D_FF = 2816
ALPHA = (2 * DEPTH) ** 0.25
BETA = (8 * DEPTH) ** -0.25
LN_EPS = 1e-5
NORM_EPS = 1e-6
NEG_INF = -1e30

SPLITS = (DN_HEADS * DN_DK, DN_HEADS * DN_DK, DN_HEADS * DN_DV, DN_HEADS * DN_DV,
          DN_HEADS, DN_HEADS, LRU_W, LRU_W, LA_HEADS * LA_HD, LA_HEADS * LA_HD,
          LA_HEADS * LA_HD, N_BRANCH * D_MODEL)
N_IN = sum(SPLITS)

kernel_name = "hybrid_deltanet_rglru_chunkattn_deepnorm"


def layer_norm(x, g, b):
    xf = x.astype(jnp.float32)
    mu = xf.mean(-1, keepdims=True)
    var = jnp.square(xf - mu).mean(-1, keepdims=True)
    return ((xf - mu) * lax.rsqrt(var + LN_EPS) * g.astype(jnp.float32)
            + b.astype(jnp.float32)).astype(x.dtype)


def swiglu(x, w12, w3):
    g, u = jnp.split(x @ w12, 2, axis=-1)
    return (jax.nn.silu(g) * u) @ w3


def causal_dwconv(x, w):
    c = x.shape[-1]
    return lax.conv_general_dilated(
        x, w[:, None, :].astype(x.dtype), window_strides=(1,),
        padding=[(CONV_W - 1, 0)], dimension_numbers=('NWC', 'WIO', 'NWC'),
        feature_group_count=c)


def l2norm(t):
    return t * lax.rsqrt(jnp.sum(t * t, axis=-1, keepdims=True) + NORM_EPS)


def chunk_gated_delta(q, k, v, g, beta):
    b_, s_, h, dk = q.shape
    dv = v.shape[-1]
    n = s_ // CHUNK

    def chunks(t):
        t = t.reshape((b_, n, CHUNK, h) + t.shape[3:])
        return jnp.moveaxis(t, (1, 3), (0, 2))

    qc, kc, vc = chunks(q), chunks(k), chunks(v)
    G = jnp.cumsum(chunks(g), axis=-1)
    bc = chunks(beta)
    idx = jnp.arange(CHUNK)
    causal = idx[:, None] >= idx[None, :]
    decay = jnp.exp(jnp.where(causal, G[..., :, None] - G[..., None, :], -jnp.inf))
    kb = kc * bc[..., None]
    m = jnp.einsum('nbhid,nbhjd->nbhij', kb, kc) * decay
    m = jnp.where(idx[:, None] > idx[None, :], m, 0.0) + jnp.eye(CHUNK, dtype=m.dtype)
    rhs = jnp.concatenate([vc * bc[..., None], kb * jnp.exp(G)[..., None]], axis=-1)
    sol = lax.linalg.triangular_solve(m, rhs, left_side=True, lower=True,
                                      unit_diagonal=True)
    u, w = sol[..., :dv], sol[..., dv:]
    qk = jnp.einsum('nbhid,nbhjd->nbhij', qc, kc) * decay
    q_dec = qc * jnp.exp(G)[..., None]
    k_dec = kc * jnp.exp(G[..., -1:] - G)[..., None]
    g_last = jnp.exp(G[..., -1])

    def step(state, inp):
        qd, kd, u_i, w_i, qk_i, gl = inp
        v_new = u_i - jnp.einsum('bhck,bhkv->bhcv', w_i, state)
        o = (jnp.einsum('bhck,bhkv->bhcv', qd, state)
             + jnp.einsum('bhij,bhjv->bhiv', qk_i, v_new))
        state = state * gl[..., None, None] + jnp.einsum('bhck,bhcv->bhkv', kd, v_new)
        return state, o

    state0 = jnp.zeros((b_, h, dk, dv), jnp.float32)
    _, o = lax.scan(step, state0, (q_dec, k_dec, u, w, qk, g_last))
    return jnp.moveaxis(o, (0, 2), (1, 3)).reshape(b_, s_, h, dv)


def gated_deltanet(q, k, v, z, b, a, conv_w, a_log, dt_bias, norm_w):
    f32 = jnp.float32
    b_, s_ = q.shape[:2]
    qkv = jax.nn.silu(causal_dwconv(jnp.concatenate([q, k, v], axis=-1), conv_w))
    q, k, v = jnp.split(qkv.astype(f32), [DN_HEADS * DN_DK, 2 * DN_HEADS * DN_DK], axis=-1)
    q = l2norm(q.reshape(b_, s_, DN_HEADS, DN_DK)) * (DN_DK ** -0.5)
    k = l2norm(k.reshape(b_, s_, DN_HEADS, DN_DK))
    v = v.reshape(b_, s_, DN_HEADS, DN_DV)
    beta = jax.nn.sigmoid(b.astype(f32))
    g = -jnp.exp(a_log.astype(f32)) * jax.nn.softplus(a.astype(f32) + dt_bias.astype(f32))
    o = chunk_gated_delta(q, k, v, g, beta)
    zf = z.reshape(b_, s_, DN_HEADS, DN_DV).astype(f32)
    o = (o * lax.rsqrt(jnp.mean(o * o, axis=-1, keepdims=True) + NORM_EPS)
         * norm_w.astype(f32) * jax.nn.silu(zf))
    return o.reshape(b_, s_, DN_HEADS * DN_DV).astype(z.dtype)


def _lin_combine(e1, e2):
    a1, b1 = e1
    a2, b2 = e2
    return a1 * a2, a2 * b1 + b2


def rglru_branch(xb, gate_in, conv_w, conv_b, gate_w, gate_b, lam):
    f32 = jnp.float32
    b_, s_ = xb.shape[:2]
    xc = (causal_dwconv(xb, conv_w) + conv_b).astype(f32)
    blk = xc.reshape(b_, s_, LRU_BLOCKS, LRU_BLK)
    gates = jnp.einsum('bsnd,gnde->gbsne', blk, gate_w.astype(f32))
    gates = gates.reshape(2, b_, s_, LRU_W) + gate_b.astype(f32)[:, None, None, :]
    r = jax.nn.sigmoid(gates[0])
    i = jax.nn.sigmoid(gates[1])
    log_a = -LRU_C * r * jax.nn.softplus(-lam.astype(f32))
    a = jnp.exp(log_a)
    u = xc * i * jnp.sqrt(-jnp.expm1(2.0 * log_a))
    _, h = lax.associative_scan(_lin_combine, (a, u), axis=1)
    return (h * jax.nn.gelu(gate_in.astype(f32))).astype(xb.dtype)


def chunk_band_attention(q, k, v, rel_table):
    b_, s_ = q.shape[:2]
    n = s_ // CHUNK
    shp = (b_, n, CHUNK, LA_HEADS, LA_HD)
    q, k, v = q.reshape(shp), k.reshape(shp), v.reshape(shp)
    pad = ((0, 0), (LA_PAST, 0), (0, 0), (0, 0), (0, 0))
    kp, vp = jnp.pad(k, pad), jnp.pad(v, pad)
    kb = jnp.concatenate([kp[:, w:w + n] for w in range(LA_PAST + 1)], axis=2)
    vb = jnp.concatenate([vp[:, w:w + n] for w in range(LA_PAST + 1)], axis=2)
    s = jnp.einsum('bnqhd,bnkhd->bhnqk', q, kb,
                   preferred_element_type=jnp.float32) * (LA_HD ** -0.5)
    qi = jnp.arange(CHUNK)[:, None]
    kj = jnp.arange(LA_BAND)[None, :]
    rel = kj - LA_PAST * CHUNK - qi
    bias = rel_table.astype(jnp.float32)[:, jnp.clip(rel, -REL_CLIP, REL_CLIP) + REL_CLIP]
    valid = (jnp.arange(n)[:, None] - LA_PAST + kj // CHUNK) >= 0
    s = jnp.where(valid[None, None, :, None, :], s + bias[None, :, None], NEG_INF)
    p = jax.nn.softmax(s, axis=-1).astype(v.dtype)
    o = jnp.einsum('bhnqk,bnkhd->bnqhd', p, vb)
    return o.reshape(b_, s_, LA_HEADS * LA_HD)


def hybrid_mixer(x, w_in, dn_conv_w, dn_a_log, dn_dt_bias, dn_norm_w, lru_conv_w,
                 lru_conv_b, lru_gate_w, lru_gate_b, lru_lambda, la_rel_bias,
                 w_branch, w_out):
    b_, s_ = x.shape[:2]
    proj = x @ w_in
    (dq, dk, dv, dz, db, da, lx, lg, aq, ak, av, gl) = jnp.split(
        proj, np.cumsum(SPLITS)[:-1].tolist(), axis=-1)
    ya = gated_deltanet(dq, dk, dv, dz, db, da, dn_conv_w, dn_a_log, dn_dt_bias, dn_norm_w)
    yb = rglru_branch(lx, lg, lru_conv_w, lru_conv_b, lru_gate_w, lru_gate_b, lru_lambda)
    yc = chunk_band_attention(aq, ak, av, la_rel_bias)
    up = jnp.einsum('bsrc,rcd->bsrd', jnp.stack([ya, yb, yc], axis=2), w_branch)
    gates = jax.nn.sigmoid(gl.astype(jnp.float32)).reshape(b_, s_, N_BRANCH, D_MODEL)
    merged = jnp.sum(gates.astype(x.dtype) * up, axis=2)
    return merged @ w_out


def memory_cross_attention(x, mem, wq, wkv, wo):
    b_, s_ = x.shape[:2]
    q = (x @ wq).reshape(b_, s_, XA_HEADS, XA_HD)
    k, v = jnp.split(mem @ wkv, 2, axis=-1)
    k = k.reshape(b_, MEM_LEN, XA_HEADS, XA_HD)
    v = v.reshape(b_, MEM_LEN, XA_HEADS, XA_HD)
    s = jnp.einsum('bshd,bmhd->bhsm', q, k,
                   preferred_element_type=jnp.float32) * (XA_HD ** -0.5)
    p = jax.nn.softmax(s, axis=-1).astype(v.dtype)
    o = jnp.einsum('bhsm,bmhd->bshd', p, v).reshape(b_, s_, D_MODEL)
    return o @ wo


def setup_inputs(seed: int = 0) -> dict:
    key = jax.random.key(seed)
    ks = jax.random.split(key, 24)
    f32 = jnp.float32

    def nrm(k, shape, fan_in, scale=1.0):
        return jax.random.normal(k, shape, f32) * (scale * fan_in ** -0.5)

    x = jax.random.normal(ks[0], (BATCH, SEQ, D_MODEL), f32)
    mem = jax.random.normal(ks[1], (BATCH, MEM_LEN, D_MODEL), f32)
    ln_g = 1.0 + 0.02 * jax.random.normal(ks[2], (DEPTH, 4, D_MODEL), f32)
    ln_b = 0.02 * jax.random.normal(ks[3], (DEPTH, 4, D_MODEL), f32)
    ffn_w12 = nrm(ks[4], (DEPTH, 2, D_MODEL, 2 * D_FF), D_MODEL)
    ffn_w3 = nrm(ks[5], (DEPTH, 2, D_FF, D_MODEL), D_FF, BETA)
    mix_w_in = nrm(ks[6], (DEPTH, D_MODEL, N_IN), D_MODEL)
    dn_conv_w = nrm(ks[7], (DEPTH, CONV_W, 2 * DN_HEADS * DN_DK + DN_HEADS * DN_DV), CONV_W)
    dn_a_log = jnp.log(jax.random.uniform(ks[8], (DEPTH, DN_HEADS), f32, 1.0, 16.0))
    dt = jnp.exp(jax.random.uniform(ks[9], (DEPTH, DN_HEADS), f32,
                                    float(np.log(1e-3)), float(np.log(1e-1))))
    dn_dt_bias = dt + jnp.log(-jnp.expm1(-dt))
    dn_norm_w = 1.0 + 0.02 * jax.random.normal(ks[10], (DEPTH, DN_DV), f32)
    lru_conv_w = nrm(ks[11], (DEPTH, CONV_W, LRU_W), CONV_W)
    lru_conv_b = 0.01 * jax.random.normal(ks[12], (DEPTH, LRU_W), f32)
    lru_gate_w = nrm(ks[13], (DEPTH, 2, LRU_BLOCKS, LRU_BLK, LRU_BLK), LRU_BLK)
    lru_gate_b = 0.01 * jax.random.normal(ks[14], (DEPTH, 2, LRU_W), f32)
    a_c = jax.random.uniform(ks[15], (DEPTH, LRU_W), f32, 0.9, 0.999)
    a0 = a_c ** (1.0 / LRU_C)
    lru_lambda = jnp.log(a0) - jnp.log1p(-a0)
    la_rel_bias = 0.1 * jax.random.normal(ks[16], (DEPTH, LA_HEADS, 2 * REL_CLIP + 1), f32)
    w_branch = nrm(ks[17], (DEPTH, N_BRANCH, BR_W, D_MODEL), BR_W)
    mix_w_out = nrm(ks[18], (DEPTH, D_MODEL, D_MODEL), D_MODEL, BETA)
    xa_wq = nrm(ks[19], (DEPTH, D_MODEL, D_MODEL), D_MODEL)
    xa_wkv = nrm(ks[20], (DEPTH, D_MODEL, 2 * D_MODEL), D_MODEL)
    xa_wo = nrm(ks[21], (DEPTH, D_MODEL, D_MODEL), D_MODEL, BETA)
    return {"x": x, "mem": mem, "ln_g": ln_g, "ln_b": ln_b, "ffn_w12": ffn_w12,
            "ffn_w3": ffn_w3, "mix_w_in": mix_w_in, "dn_conv_w": dn_conv_w,
            "dn_a_log": dn_a_log, "dn_dt_bias": dn_dt_bias, "dn_norm_w": dn_norm_w,
            "lru_conv_w": lru_conv_w, "lru_conv_b": lru_conv_b, "lru_gate_w": lru_gate_w,
            "lru_gate_b": lru_gate_b, "lru_lambda": lru_lambda, "la_rel_bias": la_rel_bias,
            "w_branch": w_branch, "mix_w_out": mix_w_out, "xa_wq": xa_wq,
            "xa_wkv": xa_wkv, "xa_wo": xa_wo}


def reference(x, mem, ln_g, ln_b, ffn_w12, ffn_w3, mix_w_in, dn_conv_w, dn_a_log,
              dn_dt_bias, dn_norm_w, lru_conv_w, lru_conv_b, lru_gate_w, lru_gate_b,
              lru_lambda, la_rel_bias, w_branch, mix_w_out, xa_wq, xa_wkv, xa_wo):
    for l in range(DEPTH):
        x = layer_norm(ALPHA * x + 0.5 * swiglu(x, ffn_w12[l, 0], ffn_w3[l, 0]),
                       ln_g[l, 0], ln_b[l, 0])
        y = hybrid_mixer(x, mix_w_in[l], dn_conv_w[l], dn_a_log[l], dn_dt_bias[l],
                         dn_norm_w[l], lru_conv_w[l], lru_conv_b[l], lru_gate_w[l],
                         lru_gate_b[l], lru_lambda[l], la_rel_bias[l], w_branch[l],
                         mix_w_out[l])
        x = layer_norm(ALPHA * x + y, ln_g[l, 1], ln_b[l, 1])
        x = layer_norm(ALPHA * x + memory_cross_attention(x, mem, xa_wq[l], xa_wkv[l], xa_wo[l]),
                       ln_g[l, 2], ln_b[l, 2])
        x = layer_norm(ALPHA * x + 0.5 * swiglu(x, ffn_w12[l, 1], ffn_w3[l, 1]),
                       ln_g[l, 3], ln_b[l, 3])
    return x
```

```python
import functools

import numpy as np
import jax
import jax.numpy as jnp
from jax import lax
from jax.experimental import pallas as pl
from jax.experimental.pallas import tpu as pltpu

F32 = jnp.float32
BF16 = jnp.bfloat16

D_MODEL = 1024
DEPTH = 2
CHUNK = 64
CONV_W = 4
BR_W = 512
N_BRANCH = 3
DN_HEADS = 4
DN_DK = 128
DN_DV = 128
LRU_W = BR_W
LRU_BLOCKS = 8
LRU_BLK = LRU_W // LRU_BLOCKS
LRU_C = 8.0
LA_HEADS = 8
LA_HD = 64
LA_PAST = 8
REL_CLIP = 128
MEM_LEN = 256
XA_HEADS = 4
XA_HD = D_MODEL // XA_HEADS
D_FF = 2816
ALPHA = (2 * DEPTH) ** 0.25
LN_EPS = 1e-5
NORM_EPS = 1e-6
NEG_INF = -1e30

LANES = 128
SUBLANES = 8
VMEM_LIMIT = 56 * 1024 * 1024

_O_DQ = 0
_O_DZ = 3 * DN_HEADS * DN_DK
_O_DB = _O_DZ + DN_HEADS * DN_DV
_O_LX = _O_DB + 2 * DN_HEADS
_O_AQ = _O_LX + 2 * LRU_W
_O_GL = _O_AQ + 3 * LA_HEADS * LA_HD
_N_IN = _O_GL + N_BRANCH * D_MODEL


def _params(*sem):
    return pltpu.CompilerParams(dimension_semantics=sem, vmem_limit_bytes=VMEM_LIMIT)


def _dot(a, b):
    return jnp.dot(a, b, preferred_element_type=F32)


def _dot_nt(a, b):
    return lax.dot_general(a, b, (((1,), (1,)), ((), ())), preferred_element_type=F32)


def _dot_tn(a, b):
    return lax.dot_general(a, b, (((0,), (0,)), ((), ())), preferred_element_type=F32)


def _split_bf16(a):
    hi = a.astype(BF16)
    lo = (a - hi.astype(F32)).astype(BF16)
    return hi, lo


def _dot3(a, b):
    ah, al = _split_bf16(a)
    bh, bl = _split_bf16(b)
    return _dot(ah, bh) + (_dot(ah, bl) + _dot(al, bh))


def _sigmoid(x):
    return 1.0 / (1.0 + jnp.exp(-x))


def _softplus(x):
    return jnp.maximum(x, 0.0) + jnp.log1p(jnp.exp(-jnp.abs(x)))


def _gelu_tanh(x):
    c = np.float32(np.sqrt(2.0 / np.pi))
    return 0.5 * x * (1.0 + jnp.tanh(c * (x + np.float32(0.044715) * (x * x * x))))


def _layer_norm(y, g, b):
    mu = jnp.mean(y, axis=-1, keepdims=True)
    d = y - mu
    var = jnp.mean(d * d, axis=-1, keepdims=True)
    return d * lax.rsqrt(var + LN_EPS) * g + b


def _ffn_kernel(x_ref, wg_ref, wu_ref, w3_ref, g_ref, b_ref, o_ref, ob_ref, xb_ref, acc_ref):
    j = pl.program_id(1)

    @pl.when(j == 0)
    def _():
        xb_ref[...] = x_ref[...].astype(BF16)
        acc_ref[...] = jnp.zeros_like(acc_ref)

    xb = xb_ref[...]
    g = _dot(xb, wg_ref[...])
    u = _dot(xb, wu_ref[...])
    h = (g * _sigmoid(g) * u).astype(BF16)
    acc_ref[...] += _dot(h, w3_ref[...])

    @pl.when(j == pl.num_programs(1) - 1)
    def _():
        y = ALPHA * x_ref[...] + 0.5 * acc_ref[...]
        out = _layer_norm(y, g_ref[...], b_ref[...])
        o_ref[...] = out
        ob_ref[...] = out.astype(BF16)


def _ffn(x, w12, w3, g, b, *, tm=1024, tf=256):
    t, d = x.shape
    nf = D_FF // tf
    return pl.pallas_call(
        _ffn_kernel,
        grid=(t // tm, nf),
        in_specs=[
            pl.BlockSpec((tm, d), lambda i, j: (i, 0)),
            pl.BlockSpec((d, tf), lambda i, j: (0, j)),
            pl.BlockSpec((d, tf), lambda i, j: (0, j + nf)),
            pl.BlockSpec((tf, d), lambda i, j: (j, 0)),
            pl.BlockSpec((1, d), lambda i, j: (0, 0)),
            pl.BlockSpec((1, d), lambda i, j: (0, 0)),
        ],
        out_specs=[
            pl.BlockSpec((tm, d), lambda i, j: (i, 0)),
            pl.BlockSpec((tm, d), lambda i, j: (i, 0)),
        ],
        out_shape=[jax.ShapeDtypeStruct((t, d), F32), jax.ShapeDtypeStruct((t, d), BF16)],
        scratch_shapes=[pltpu.VMEM((tm, d), BF16), pltpu.VMEM((tm, d), F32)],
        compiler_params=_params("parallel", "arbitrary"),
        name="ffn_ln",
    )(x, w12, w12, w3, g, b)


def _causal_conv(y, tail, cw):
    rows = y.shape[0]
    row8 = lax.broadcasted_iota(jnp.int32, (SUBLANES, y.shape[1]), 0)
    acc = y * cw[CONV_W - 1:CONV_W, :]
    fix = jnp.zeros((SUBLANES, y.shape[1]), F32)
    for k in range(1, CONV_W):
        wk = cw[CONV_W - 1 - k:CONV_W - k, :]
        sh = pltpu.roll(y, k, axis=0)
        acc = acc + sh * wk
        prev = pltpu.roll(tail, k, axis=0)
        fix = fix + jnp.where(row8 < k, (prev - sh[:SUBLANES]) * wk, 0.0)
    return jnp.concatenate([acc[:SUBLANES] + fix, acc[SUBLANES:]], axis=0)


def _proj_qkvz_kernel(tiles_per_seq, x_ref, w_ref, cw_ref, o_ref, tail_ref):
    i = pl.program_id(0)
    j = pl.program_id(1)
    y = _dot(x_ref[...], w_ref[...])

    @pl.when(j < 3)
    def _():
        first = (i % tiles_per_seq) == 0
        tail = jnp.where(first, 0.0, tail_ref[j])
        tail_ref[j] = y[y.shape[0] - SUBLANES:]
        c = _causal_conv(y, tail, cw_ref[...])
        c = c * _sigmoid(c)
        scale = jnp.where(j == 0, np.float32(DN_DK ** -0.5), np.float32(1.0))
        parts = []
        for h in range(DN_HEADS):
            ch = c[:, h * DN_DK:(h + 1) * DN_DK]
            nrm = lax.rsqrt(jnp.sum(ch * ch, axis=-1, keepdims=True) + NORM_EPS) * scale
            parts.append(ch * nrm)
        normed = jnp.concatenate(parts, axis=1)
        o_ref[...] = jnp.where(j == 2, c, normed)

    @pl.when(j == 3)
    def _():
        o_ref[...] = y


def _proj_qkvz(xb, w, cw, *, seq, tm=512):
    t, d = xb.shape
    tn = DN_HEADS * DN_DK
    return pl.pallas_call(
        functools.partial(_proj_qkvz_kernel, seq // tm),
        grid=(t // tm, 4),
        in_specs=[
            pl.BlockSpec((tm, d), lambda i, j: (i, 0)),
            pl.BlockSpec((d, tn), lambda i, j: (0, j)),
            pl.BlockSpec((CONV_W, tn), lambda i, j: (0, jnp.minimum(j, 2))),
        ],
        out_specs=pl.BlockSpec((tm, tn), lambda i, j: (i, j)),
        out_shape=jax.ShapeDtypeStruct((t, 4 * tn), F32),
        scratch_shapes=[pltpu.VMEM((3, SUBLANES, tn), F32)],
        compiler_params=_params("arbitrary", "arbitrary"),
        name="proj_qkvz",
    )(xb, w, cw)


def _proj_lru_kernel(tiles_per_seq, x_ref, w_ref, cw_ref, cb_ref, o_ref, tail_ref):
    i = pl.program_id(0)
    j = pl.program_id(1)
    y = _dot(x_ref[...], w_ref[...])

    @pl.when(j == 0)
    def _():
        first = (i % tiles_per_seq) == 0
        tail = jnp.where(first, 0.0, tail_ref[...])
        tail_ref[...] = y[y.shape[0] - SUBLANES:]
        o_ref[...] = _causal_conv(y, tail, cw_ref[...]) + cb_ref[...]

    @pl.when(j == 1)
    def _():
        o_ref[...] = _gelu_tanh(y)


def _proj_lru(xb, w, cw, cb, *, seq, tm=512):
    t, d = xb.shape
    tn = LRU_W
    return pl.pallas_call(
        functools.partial(_proj_lru_kernel, seq // tm),
        grid=(t // tm, 2),
        in_specs=[
            pl.BlockSpec((tm, d), lambda i, j: (i, 0)),
            pl.BlockSpec((d, tn), lambda i, j: (0, j)),
            pl.BlockSpec((CONV_W, tn), lambda i, j: (0, 0)),
            pl.BlockSpec((1, tn), lambda i, j: (0, 0)),
        ],
        out_specs=pl.BlockSpec((tm, tn), lambda i, j: (i, j)),
        out_shape=jax.ShapeDtypeStruct((t, 2 * tn), F32),
        scratch_shapes=[pltpu.VMEM((SUBLANES, tn), F32)],
        compiler_params=_params("arbitrary", "arbitrary"),
        name="proj_lru",
    )(xb, w, cw, cb)


def _proj_plain_kernel(act, x_ref, w_ref, o_ref):
    y = _dot(x_ref[...], w_ref[...])
    if act == "sigmoid":
        y = _sigmoid(y)
    o_ref[...] = y.astype(o_ref.dtype)


def _proj_plain(xb, w, *, tn, out_dtype, act=None, tm=512, name="proj"):
    t, d = xb.shape
    n = w.shape[1]
    tm = min(tm, t)
    return pl.pallas_call(
        functools.partial(_proj_plain_kernel, act),
        grid=(t // tm, n // tn),
        in_specs=[
            pl.BlockSpec((tm, d), lambda i, j: (i, 0)),
            pl.BlockSpec((d, tn), lambda i, j: (0, j)),
        ],
        out_specs=pl.BlockSpec((tm, tn), lambda i, j: (i, j)),
        out_shape=jax.ShapeDtypeStruct((t, n), out_dtype),
        compiler_params=_params("parallel", "parallel"),
        name=name,
    )(xb, w)


def _cumsum_rows(x):
    rows = x.shape[0]
    row = lax.broadcasted_iota(jnp.int32, x.shape, 0)
    d = 1
    while d < rows:
        x = x + jnp.where(row >= d, pltpu.roll(x, d, axis=0), 0.0)
        d *= 2
    return x


def _delta_kernel(batch, qkvz_ref, ba_ref, alog_ref, dtb_ref, nw_ref, o_ref, state_ref):
    c = pl.program_id(0)

    @pl.when(c == 0)
    def _():
        state_ref[...] = jnp.zeros_like(state_ref)

    row = lax.broadcasted_iota(jnp.int32, (CHUNK, CHUNK), 0)
    col = lax.broadcasted_iota(jnp.int32, (CHUNK, CHUNK), 1)
    lower = row >= col
    strict = row > col
    eye = jnp.where(row == col, 1.0, 0.0).astype(F32)
    hk = DN_HEADS * DN_DK

    for b in range(batch):
        ba = ba_ref[b]
        beta_all = _sigmoid(ba)
        g_all = -jnp.exp(alog_ref[...]) * _softplus(ba + dtb_ref[...])
        g_cum = _cumsum_rows(g_all)
        g_cum_t = jnp.transpose(
            jnp.concatenate([g_cum, jnp.zeros((LANES - CHUNK, LANES), F32)], axis=0))
        for h in range(DN_HEADS):
            q = qkvz_ref[b, :, h * DN_DK:(h + 1) * DN_DK]
            k = qkvz_ref[b, :, hk + h * DN_DK:hk + (h + 1) * DN_DK]
            v = qkvz_ref[b, :, 2 * hk + h * DN_DV:2 * hk + (h + 1) * DN_DV]
            z = qkvz_ref[b, :, 3 * hk + h * DN_DV:3 * hk + (h + 1) * DN_DV]
            beta = beta_all[:, h:h + 1]
            gc = g_cum[:, DN_HEADS + h:DN_HEADS + h + 1]
            gr = g_cum_t[DN_HEADS + h:DN_HEADS + h + 1, :CHUNK]
            decay = jnp.exp(jnp.where(lower, gc - gr, NEG_INF))
            kb = k * beta
            kb16 = kb.astype(BF16)
            k16 = k.astype(BF16)
            lmat = jnp.where(strict, _dot_nt(kb16, k16) * decay, 0.0)
            exp_g = jnp.exp(gc)
            rhs = jnp.concatenate([v * beta, kb * exp_g], axis=1)
            p = -lmat
            tinv = eye + p
            n = 1
            while 2 * n < CHUNK:
                p = _dot3(p, p)
                tinv = tinv + _dot3(tinv, p)
                n *= 2
            sol = _dot3(tinv, rhs)
            u = sol[:, :DN_DV]
            w = sol[:, DN_DV:]
            qk = _dot_nt(q.astype(BF16), k16) * decay
            g_last = gc[CHUNK - 1:CHUNK, :]
            q_dec = q * exp_g
            k_dec = k * jnp.exp(g_last - gc)
            s = state_ref[b * DN_HEADS + h]
            s16 = s.astype(BF16)
            v_new = u - _dot(w.astype(BF16), s16)
            v_new16 = v_new.astype(BF16)
            o = _dot(q_dec.astype(BF16), s16) + _dot(qk.astype(BF16), v_new16)
            state_ref[b * DN_HEADS + h] = s * jnp.exp(g_last) + _dot_tn(k_dec.astype(BF16), v_new16)
            o = o * lax.rsqrt(jnp.mean(o * o, axis=-1, keepdims=True) + NORM_EPS)
            o = o * nw_ref[...] * (z * _sigmoid(z))
            o_ref[b, :, h * DN_DV:(h + 1) * DN_DV] = o.astype(o_ref.dtype)


def _delta(qkvz, ba, alog, dtb, nw):
    batch, seq, width = qkvz.shape
    return pl.pallas_call(
        functools.partial(_delta_kernel, batch),
        grid=(seq // CHUNK,),
        in_specs=[
            pl.BlockSpec((batch, CHUNK, width), lambda c: (0, c, 0)),
            pl.BlockSpec((batch, CHUNK, LANES), lambda c: (0, c, 0)),
            pl.BlockSpec((1, LANES), lambda c: (0, 0)),
            pl.BlockSpec((1, LANES), lambda c: (0, 0)),
            pl.BlockSpec((1, DN_DV), lambda c: (0, 0)),
        ],
        out_specs=pl.BlockSpec((batch, CHUNK, DN_HEADS * DN_DV), lambda c: (0, c, 0)),
        out_shape=jax.ShapeDtypeStruct((batch, seq, DN_HEADS * DN_DV), BF16),
        scratch_shapes=[pltpu.VMEM((batch * DN_HEADS, DN_DK, DN_DV), F32)],
        compiler_params=_params("arbitrary"),
        name="gated_delta",
    )(qkvz, ba, alog, dtb, nw)


def _lru_kernel(tiles_per_seq, xg_ref, wg_ref, gb_ref, lam_ref, o_ref, h_ref):
    i = pl.program_id(0)
    rows = xg_ref.shape[0]
    xc = xg_ref[:, :LRU_W]
    gate = xg_ref[:, LRU_W:]
    gates = _dot(xc.astype(BF16), wg_ref[...]) + gb_ref[...]
    r = _sigmoid(gates[:, :LRU_W])
    ig = _sigmoid(gates[:, LRU_W:])
    log_a = -LRU_C * r * _softplus(-lam_ref[...])
    a = jnp.exp(log_a)
    u = xc * ig * jnp.sqrt(-jnp.tanh(log_a) * (1.0 + a * a))
    row = lax.broadcasted_iota(jnp.int32, a.shape, 0)
    d = 1
    while d < rows:
        a_sh = jnp.where(row >= d, pltpu.roll(a, d, axis=0), 1.0)
        u_sh = jnp.where(row >= d, pltpu.roll(u, d, axis=0), 0.0)
        u = a * u_sh + u
        a = a * a_sh
        d *= 2
    first = (i % tiles_per_seq) == 0
    h_prev = jnp.where(first, 0.0, h_ref[...])
    h = u + a * h_prev
    h_ref[...] = h[rows - 1:rows, :]
    o_ref[...] = (h * gate).astype(o_ref.dtype)


def _lru(xg, wg, gb, lam, *, seq, tm=256):
    t = xg.shape[0]
    return pl.pallas_call(
        functools.partial(_lru_kernel, seq // tm),
        grid=(t // tm,),
        in_specs=[
            pl.BlockSpec((tm, 2 * LRU_W), lambda i: (i, 0)),
            pl.BlockSpec((LRU_W, 2 * LRU_W), lambda i: (0, 0)),
            pl.BlockSpec((1, 2 * LRU_W), lambda i: (0, 0)),
            pl.BlockSpec((1, LRU_W), lambda i: (0, 0)),
        ],
        out_specs=pl.BlockSpec((tm, LRU_W), lambda i: (i, 0)),
        out_shape=jax.ShapeDtypeStruct((t, LRU_W), BF16),
        scratch_shapes=[pltpu.VMEM((1, LRU_W), F32)],
        compiler_params=_params("arbitrary"),
        name="rg_lru",
    )(xg, wg, gb, lam)


_QB = 4 * CHUNK
_KB = 3


def _band_kernel(q_ref, k0_ref, k1_ref, k2_ref, v0_ref, v1_ref, v2_ref, bias_ref, o_ref):
    i = pl.program_id(1)
    k_refs = (k0_ref, k1_ref, k2_ref)
    v_refs = (v0_ref, v1_ref, v2_ref)
    pen = [jnp.where(i >= _KB - 1 - m, 0.0, NEG_INF).astype(F32) for m in range(_KB)]
    for h in range(LA_HEADS):
        sl = slice(h * LA_HD, (h + 1) * LA_HD)
        q = q_ref[0, :, sl]
        s = [_dot_nt(q, k_refs[m][0, :, sl]) + bias_ref[h, m] + pen[m] for m in range(_KB)]
        mx = s[0].max(axis=-1, keepdims=True)
        for m in range(1, _KB):
            mx = jnp.maximum(mx, s[m].max(axis=-1, keepdims=True))
        p = [jnp.exp(sm - mx) for sm in s]
        den = p[0].sum(axis=-1, keepdims=True)
        for m in range(1, _KB):
            den = den + p[m].sum(axis=-1, keepdims=True)
        acc = _dot(p[0].astype(BF16), v_refs[0][0, :, sl])
        for m in range(1, _KB):
            acc = acc + _dot(p[m].astype(BF16), v_refs[m][0, :, sl])
        o_ref[0, :, sl] = (acc / den).astype(o_ref.dtype)


def _band_attention(qkv, bias, *, batch, seq):
    hw = LA_HEADS * LA_HD
    nblk = seq // _QB

    def kv_spec(col, m):
        return pl.BlockSpec((1, _QB, hw), lambda b, i: (b, jnp.maximum(i - (_KB - 1 - m), 0), col))

    return pl.pallas_call(
        _band_kernel,
        grid=(batch, nblk),
        in_specs=[pl.BlockSpec((1, _QB, hw), lambda b, i: (b, i, 0))]
        + [kv_spec(1, m) for m in range(_KB)]
        + [kv_spec(2, m) for m in range(_KB)]
        + [pl.BlockSpec((LA_HEADS, _KB, _QB, _QB), lambda b, i: (0, 0, 0, 0))],
        out_specs=pl.BlockSpec((1, _QB, hw), lambda b, i: (b, i, 0)),
        out_shape=jax.ShapeDtypeStruct((batch, seq, hw), BF16),
        compiler_params=_params("parallel", "arbitrary"),
        name="band_attention",
    )(qkv, qkv, qkv, qkv, qkv, qkv, qkv, bias)


def _band_bias(rel_table):
    qpos = np.arange(_QB)[:, None]
    m_idx = np.arange(_KB)[:, None, None]
    kpos = np.arange(_QB)[None, :] + (m_idx - (_KB - 1)) * _QB
    rel = kpos - qpos
    chunk_off = kpos // CHUNK - qpos // CHUNK
    valid = (chunk_off <= 0) & (chunk_off >= -LA_PAST)
    idx = np.clip(rel, -REL_CLIP, REL_CLIP) + REL_CLIP
    bias = rel_table.astype(F32)[:, idx]
    return jnp.where(jnp.asarray(valid)[None], bias, NEG_INF)


def _mix_out_kernel(ya_ref, yb_ref, yc_ref, gt_ref, x_ref, wb_ref, wo_ref, g_ref, b_ref, o_ref, ob_ref):
    ys = (ya_ref, yb_ref, yc_ref)
    merged = None
    for r in range(N_BRANCH):
        up = _dot(ys[r][...], wb_ref[r])
        term = gt_ref[:, r * D_MODEL:(r + 1) * D_MODEL] * up
        merged = term if merged is None else merged + term
    y = _dot(merged.astype(BF16), wo_ref[...])
    out = _layer_norm(ALPHA * x_ref[...] + y, g_ref[...], b_ref[...])
    o_ref[...] = out
    ob_ref[...] = out.astype(BF16)


def _mix_out(ya, yb, yc, gates, x, wb, wo, g, b, *, tm=512):
    t, d = x.shape
    return pl.pallas_call(
        _mix_out_kernel,
        grid=(t // tm,),
        in_specs=[
            pl.BlockSpec((tm, BR_W), lambda i: (i, 0)),
            pl.BlockSpec((tm, BR_W), lambda i: (i, 0)),
            pl.BlockSpec((tm, BR_W), lambda i: (i, 0)),
            pl.BlockSpec((tm, N_BRANCH * d), lambda i: (i, 0)),
            pl.BlockSpec((tm, d), lambda i: (i, 0)),
            pl.BlockSpec((N_BRANCH, BR_W, d), lambda i: (0, 0, 0)),
            pl.BlockSpec((d, d), lambda i: (0, 0)),
            pl.BlockSpec((1, d), lambda i: (0, 0)),
            pl.BlockSpec((1, d), lambda i: (0, 0)),
        ],
        out_specs=[pl.BlockSpec((tm, d), lambda i: (i, 0)), pl.BlockSpec((tm, d), lambda i: (i, 0))],
        out_shape=[jax.ShapeDtypeStruct((t, d), F32), jax.ShapeDtypeStruct((t, d), BF16)],
        compiler_params=_params("parallel"),
        name="mix_out",
    )(ya, yb, yc, gates, x, wb, wo, g, b)


def _xattn_kernel(xb_ref, x_ref, kv_ref, wq_ref, wo_ref, g_ref, b_ref, o_ref, ob_ref):
    q = (_dot(xb_ref[0], wq_ref[...]) * np.float32(XA_HD ** -0.5)).astype(BF16)
    outs = []
    for h in range(XA_HEADS):
        sl = slice(h * XA_HD, (h + 1) * XA_HD)
        k = kv_ref[0, :, sl]
        v = kv_ref[0, :, D_MODEL + h * XA_HD:D_MODEL + (h + 1) * XA_HD]
        s = _dot_nt(q[:, sl], k)
        p = jnp.exp(s - s.max(axis=-1, keepdims=True))
        den = p.sum(axis=-1, keepdims=True)
        outs.append((_dot(p.astype(BF16), v) / den).astype(BF16))
    o = jnp.concatenate(outs, axis=1)
    y = _dot(o, wo_ref[...])
    out = _layer_norm(ALPHA * x_ref[0] + y, g_ref[...], b_ref[...])
    o_ref[0] = out
    ob_ref[0] = out.astype(BF16)


def _xattn(xb, x, kv, wq, wo, g, b, *, tm=512):
    batch, seq, d = x.shape
    return pl.pallas_call(
        _xattn_kernel,
        grid=(batch, seq // tm),
        in_specs=[
            pl.BlockSpec((1, tm, d), lambda bi, i: (bi, i, 0)),
            pl.BlockSpec((1, tm, d), lambda bi, i: (bi, i, 0)),
            pl.BlockSpec((1, MEM_LEN, 2 * d), lambda bi, i: (bi, 0, 0)),
            pl.BlockSpec((d, d), lambda bi, i: (0, 0)),
            pl.BlockSpec((d, d), lambda bi, i: (0, 0)),
            pl.BlockSpec((1, d), lambda bi, i: (0, 0)),
            pl.BlockSpec((1, d), lambda bi, i: (0, 0)),
        ],
        out_specs=[
            pl.BlockSpec((1, tm, d), lambda bi, i: (bi, i, 0)),
            pl.BlockSpec((1, tm, d), lambda bi, i: (bi, i, 0)),
        ],
        out_shape=[jax.ShapeDtypeStruct((batch, seq, d), F32), jax.ShapeDtypeStruct((batch, seq, d), BF16)],
        compiler_params=_params("parallel", "parallel"),
        name="mem_xattn",
    )(xb, x, kv, wq, wo, g, b)


def _pad_lanes(v, offset):
    out = jnp.zeros((1, LANES), F32)
    return out.at[0, offset:offset + v.shape[0]].set(v.astype(F32))


def _layer(x, mem_b, l, p):
    batch, seq, d = x.shape
    t = batch * seq
    row = lambda v: v.reshape(1, -1).astype(F32)

    x, xb = _ffn(x.reshape(t, d), p["ffn_w12"][l, 0].astype(BF16), p["ffn_w3"][l, 0].astype(BF16),
                 row(p["ln_g"][l, 0]), row(p["ln_b"][l, 0]))

    w_in = p["mix_w_in"][l]
    qkvz = _proj_qkvz(xb, w_in[:, _O_DQ:_O_DB].astype(BF16), p["dn_conv_w"][l].astype(F32), seq=seq)
    w_ba = jnp.zeros((d, LANES), F32).at[:, :2 * DN_HEADS].set(w_in[:, _O_DB:_O_LX]).astype(BF16)
    ba = _proj_plain(xb, w_ba, tn=LANES, out_dtype=F32, name="proj_ba")
    ya = _delta(qkvz.reshape(batch, seq, -1), ba.reshape(batch, seq, LANES),
                _pad_lanes(p["dn_a_log"][l], DN_HEADS), _pad_lanes(p["dn_dt_bias"][l], DN_HEADS),
                row(p["dn_norm_w"][l]))

    xg = _proj_lru(xb, w_in[:, _O_LX:_O_AQ].astype(BF16), p["lru_conv_w"][l].astype(F32),
                   row(p["lru_conv_b"][l]), seq=seq)
    gw = p["lru_gate_w"][l].astype(F32)
    eye = jnp.eye(LRU_BLOCKS, dtype=F32)
    wg = jnp.einsum("gnde,nm->gndme", gw, eye).reshape(2, LRU_W, LRU_W)
    wg = jnp.concatenate([wg[0], wg[1]], axis=1).astype(BF16)
    yb = _lru(xg, wg, p["lru_gate_b"][l].reshape(1, -1).astype(F32), row(p["lru_lambda"][l]), seq=seq)

    w_att = w_in[:, _O_AQ:_O_GL]
    w_att = jnp.concatenate([w_att[:, :LA_HEADS * LA_HD] * np.float32(LA_HD ** -0.5),
                             w_att[:, LA_HEADS * LA_HD:]], axis=1).astype(BF16)
    aqkv = _proj_plain(xb, w_att, tn=LA_HEADS * LA_HD, out_dtype=BF16, name="proj_att")
    yc = _band_attention(aqkv.reshape(batch, seq, -1), _band_bias(p["la_rel_bias"][l]), batch=batch, seq=seq)

    gates = _proj_plain(xb, w_in[:, _O_GL:].astype(BF16), tn=D_MODEL, out_dtype=F32, act="sigmoid",
                        name="proj_gates")
    x, xb = _mix_out(ya.reshape(t, -1), yb, yc.reshape(t, -1), gates, x,
                     p["w_branch"][l].astype(BF16), p["mix_w_out"][l].astype(BF16),
                     row(p["ln_g"][l, 1]), row(p["ln_b"][l, 1]))

    kv = _proj_plain(mem_b, p["xa_wkv"][l].astype(BF16), tn=D_MODEL, out_dtype=BF16, name="proj_kv")
    x, xb = _xattn(xb.reshape(batch, seq, d), x.reshape(batch, seq, d), kv.reshape(batch, MEM_LEN, 2 * d),
                   p["xa_wq"][l].astype(BF16), p["xa_wo"][l].astype(BF16),
                   row(p["ln_g"][l, 2]), row(p["ln_b"][l, 2]))

    x, _ = _ffn(x.reshape(t, d), p["ffn_w12"][l, 1].astype(BF16), p["ffn_w3"][l, 1].astype(BF16),
                row(p["ln_g"][l, 3]), row(p["ln_b"][l, 3]))
    return x.reshape(batch, seq, d)


def kernel(x, mem, ln_g, ln_b, ffn_w12, ffn_w3, mix_w_in, dn_conv_w, dn_a_log, dn_dt_bias, dn_norm_w,
           lru_conv_w, lru_conv_b, lru_gate_w, lru_gate_b, lru_lambda, la_rel_bias, w_branch, mix_w_out,
           xa_wq, xa_wkv, xa_wo):
    p = dict(ln_g=ln_g, ln_b=ln_b, ffn_w12=ffn_w12, ffn_w3=ffn_w3, mix_w_in=mix_w_in, dn_conv_w=dn_conv_w,
             dn_a_log=dn_a_log, dn_dt_bias=dn_dt_bias, dn_norm_w=dn_norm_w, lru_conv_w=lru_conv_w,
             lru_conv_b=lru_conv_b, lru_gate_w=lru_gate_w, lru_gate_b=lru_gate_b, lru_lambda=lru_lambda,
             la_rel_bias=la_rel_bias, w_branch=w_branch, mix_w_out=mix_w_out, xa_wq=xa_wq, xa_wkv=xa_wkv,
             xa_wo=xa_wo)
    batch = x.shape[0]
    mem_b = mem.reshape(batch * MEM_LEN, D_MODEL).astype(BF16)
    x = x.astype(F32)
    for l in range(DEPTH):
        x = _layer(x, mem_b, l, p)
    return x
```

```python
import functools

import numpy as np
import jax
import jax.numpy as jnp
from jax import lax
from jax.experimental import pallas as pl
from jax.experimental.pallas import tpu as pltpu

F32 = jnp.float32
BF16 = jnp.bfloat16

D_MODEL = 1024
DEPTH = 2
CHUNK = 64
CONV_W = 4
BR_W = 512
N_BRANCH = 3
DN_HEADS = 4
DN_DK = 128
DN_DV = 128
LRU_W = BR_W
LRU_BLOCKS = 8
LRU_BLK = LRU_W // LRU_BLOCKS
LRU_C = 8.0
LA_HEADS = 8
LA_HD = 64
LA_PAST = 8
REL_CLIP = 128
MEM_LEN = 256
XA_HEADS = 4
XA_HD = D_MODEL // XA_HEADS
D_FF = 2816
ALPHA = (2 * DEPTH) ** 0.25
LN_EPS = 1e-5
NORM_EPS = 1e-6
NEG_INF = -1e30

LANES = 128
SUBLANES = 8
VMEM_LIMIT = 56 * 1024 * 1024

_O_DQ = 0
_O_DZ = 3 * DN_HEADS * DN_DK
_O_DB = _O_DZ + DN_HEADS * DN_DV
_O_LX = _O_DB + 2 * DN_HEADS
_O_AQ = _O_LX + 2 * LRU_W
_O_GL = _O_AQ + 3 * LA_HEADS * LA_HD
_N_IN = _O_GL + N_BRANCH * D_MODEL


def _params(*sem):
    return pltpu.CompilerParams(dimension_semantics=sem, vmem_limit_bytes=VMEM_LIMIT)


def _dot(a, b):
    return jnp.dot(a, b, preferred_element_type=F32)


def _dot_nt(a, b):
    return lax.dot_general(a, b, (((1,), (1,)), ((), ())), preferred_element_type=F32)


def _dot_tn(a, b):
    return lax.dot_general(a, b, (((0,), (0,)), ((), ())), preferred_element_type=F32)


def _split_bf16(a):
    hi = a.astype(BF16)
    lo = (a - hi.astype(F32)).astype(BF16)
    return hi, lo


def _dot3(a, b):
    ah, al = _split_bf16(a)
    bh, bl = _split_bf16(b)
    return _dot(ah, bh) + (_dot(ah, bl) + _dot(al, bh))


def _sigmoid(x):
    return 1.0 / (1.0 + jnp.exp(-x))


def _softplus(x):
    return jnp.maximum(x, 0.0) + jnp.log1p(jnp.exp(-jnp.abs(x)))


def _gelu_tanh(x):
    c = np.float32(np.sqrt(2.0 / np.pi))
    return 0.5 * x * (1.0 + jnp.tanh(c * (x + np.float32(0.044715) * (x * x * x))))


def _layer_norm(y, g, b):
    mu = jnp.mean(y, axis=-1, keepdims=True)
    d = y - mu
    var = jnp.mean(d * d, axis=-1, keepdims=True)
    return d * lax.rsqrt(var + LN_EPS) * g + b


def _ffn_kernel(x_ref, wg_ref, wu_ref, w3_ref, g_ref, b_ref, o_ref, ob_ref, xb_ref, acc_ref):
    j = pl.program_id(1)

    @pl.when(j == 0)
    def _():
        xb_ref[...] = x_ref[...].astype(BF16)
        acc_ref[...] = jnp.zeros_like(acc_ref)

    xb = xb_ref[...]
    g = _dot(xb, wg_ref[...])
    u = _dot(xb, wu_ref[...])
    h = (g * _sigmoid(g) * u).astype(BF16)
    acc_ref[...] += _dot(h, w3_ref[...])

    @pl.when(j == pl.num_programs(1) - 1)
    def _():
        y = ALPHA * x_ref[...] + 0.5 * acc_ref[...]
        out = _layer_norm(y, g_ref[...], b_ref[...])
        o_ref[...] = out
        ob_ref[...] = out.astype(BF16)


def _ffn(x, w12, w3, g, b, *, tm=1024, tf=256):
    t, d = x.shape
    nf = D_FF // tf
    return pl.pallas_call(
        _ffn_kernel,
        grid=(t // tm, nf),
        in_specs=[
            pl.BlockSpec((tm, d), lambda i, j: (i, 0)),
            pl.BlockSpec((d, tf), lambda i, j: (0, j)),
            pl.BlockSpec((d, tf), lambda i, j: (0, j + nf)),
            pl.BlockSpec((tf, d), lambda i, j: (j, 0)),
            pl.BlockSpec((1, d), lambda i, j: (0, 0)),
            pl.BlockSpec((1, d), lambda i, j: (0, 0)),
        ],
        out_specs=[
            pl.BlockSpec((tm, d), lambda i, j: (i, 0)),
            pl.BlockSpec((tm, d), lambda i, j: (i, 0)),
        ],
        out_shape=[jax.ShapeDtypeStruct((t, d), F32), jax.ShapeDtypeStruct((t, d), BF16)],
        scratch_shapes=[pltpu.VMEM((tm, d), BF16), pltpu.VMEM((tm, d), F32)],
        compiler_params=_params("parallel", "arbitrary"),
        name="ffn_ln",
    )(x, w12, w12, w3, g, b)


def _causal_conv(y, tail, cw):
    rows = y.shape[0]
    row8 = lax.broadcasted_iota(jnp.int32, (SUBLANES, y.shape[1]), 0)
    acc = y * cw[CONV_W - 1:CONV_W, :]
    fix = jnp.zeros((SUBLANES, y.shape[1]), F32)
    for k in range(1, CONV_W):
        wk = cw[CONV_W - 1 - k:CONV_W - k, :]
        sh = pltpu.roll(y, k, axis=0)
        acc = acc + sh * wk
        prev = pltpu.roll(tail, k, axis=0)
        fix = fix + jnp.where(row8 < k, (prev - sh[:SUBLANES]) * wk, 0.0)
    return jnp.concatenate([acc[:SUBLANES] + fix, acc[SUBLANES:]], axis=0)


def _proj_qkvz_kernel(tiles_per_seq, x_ref, w_ref, cw_ref, o_ref, tail_ref):
    i = pl.program_id(0)
    j = pl.program_id(1)
    y = _dot(x_ref[...], w_ref[...])

    @pl.when(j < 3)
    def _():
        first = (i % tiles_per_seq) == 0
        tail = jnp.where(first, 0.0, tail_ref[j])
        tail_ref[j] = y[y.shape[0] - SUBLANES:]
        c = _causal_conv(y, tail, cw_ref[...])
        c = c * _sigmoid(c)
        scale = jnp.where(j == 0, np.float32(DN_DK ** -0.5), np.float32(1.0))
        parts = []
        for h in range(DN_HEADS):
            ch = c[:, h * DN_DK:(h + 1) * DN_DK]
            nrm = lax.rsqrt(jnp.sum(ch * ch, axis=-1, keepdims=True) + NORM_EPS) * scale
            parts.append(ch * nrm)
        normed = jnp.concatenate(parts, axis=1)
        o_ref[...] = jnp.where(j == 2, c, normed)

    @pl.when(j == 3)
    def _():
        o_ref[...] = y


def _proj_qkvz(xb, w, cw, *, seq, tm=512):
    t, d = xb.shape
    tn = DN_HEADS * DN_DK
    return pl.pallas_call(
        functools.partial(_proj_qkvz_kernel, seq // tm),
        grid=(t // tm, 4),
        in_specs=[
            pl.BlockSpec((tm, d), lambda i, j: (i, 0)),
            pl.BlockSpec((d, tn), lambda i, j: (0, j)),
            pl.BlockSpec((CONV_W, tn), lambda i, j: (0, jnp.minimum(j, 2))),
        ],
        out_specs=pl.BlockSpec((tm, tn), lambda i, j: (i, j)),
        out_shape=jax.ShapeDtypeStruct((t, 4 * tn), F32),
        scratch_shapes=[pltpu.VMEM((3, SUBLANES, tn), F32)],
        compiler_params=_params("arbitrary", "arbitrary"),
        name="proj_qkvz",
    )(xb, w, cw)


def _proj_lru_kernel(tiles_per_seq, x_ref, w_ref, cw_ref, cb_ref, o_ref, tail_ref):
    i = pl.program_id(0)
    j = pl.program_id(1)
    y = _dot(x_ref[...], w_ref[...])

    @pl.when(j == 0)
    def _():
        first = (i % tiles_per_seq) == 0
        tail = jnp.where(first, 0.0, tail_ref[...])
        tail_ref[...] = y[y.shape[0] - SUBLANES:]
        o_ref[...] = _causal_conv(y, tail, cw_ref[...]) + cb_ref[...]

    @pl.when(j == 1)
    def _():
        o_ref[...] = _gelu_tanh(y)


def _proj_lru(xb, w, cw, cb, *, seq, tm=512):
    t, d = xb.shape
    tn = LRU_W
    return pl.pallas_call(
        functools.partial(_proj_lru_kernel, seq // tm),
        grid=(t // tm, 2),
        in_specs=[
            pl.BlockSpec((tm, d), lambda i, j: (i, 0)),
            pl.BlockSpec((d, tn), lambda i, j: (0, j)),
            pl.BlockSpec((CONV_W, tn), lambda i, j: (0, 0)),
            pl.BlockSpec((1, tn), lambda i, j: (0, 0)),
        ],
        out_specs=pl.BlockSpec((tm, tn), lambda i, j: (i, j)),
        out_shape=jax.ShapeDtypeStruct((t, 2 * tn), F32),
        scratch_shapes=[pltpu.VMEM((SUBLANES, tn), F32)],
        compiler_params=_params("arbitrary", "arbitrary"),
        name="proj_lru",
    )(xb, w, cw, cb)


def _proj_plain_kernel(act, x_ref, w_ref, o_ref):
    y = _dot(x_ref[...], w_ref[...])
    if act == "sigmoid":
        y = _sigmoid(y)
    o_ref[...] = y.astype(o_ref.dtype)


def _proj_plain(xb, w, *, tn, out_dtype, act=None, tm=512, name="proj"):
    t, d = xb.shape
    n = w.shape[1]
    tm = min(tm, t)
    return pl.pallas_call(
        functools.partial(_proj_plain_kernel, act),
        grid=(t // tm, n // tn),
        in_specs=[
            pl.BlockSpec((tm, d), lambda i, j: (i, 0)),
            pl.BlockSpec((d, tn), lambda i, j: (0, j)),
        ],
        out_specs=pl.BlockSpec((tm, tn), lambda i, j: (i, j)),
        out_shape=jax.ShapeDtypeStruct((t, n), out_dtype),
        compiler_params=_params("parallel", "parallel"),
        name=name,
    )(xb, w)


_PREP_CHUNKS = 4


def _segmented_cumsum_rows(x, seg):
    row = lax.broadcasted_iota(jnp.int32, x.shape, 0) & (seg - 1)
    d = 1
    while d < seg:
        x = x + jnp.where(row >= d, pltpu.roll(x, d, axis=0), 0.0)
        d *= 2
    return x


def _delta_prep_kernel(q_ref, k_ref, v_ref, ba_ref, alog_ref, dtb_ref,
                       wq_ref, u_ref, qk_ref, kdt_ref, gl_ref):
    row = lax.broadcasted_iota(jnp.int32, (CHUNK, CHUNK), 0)
    col = lax.broadcasted_iota(jnp.int32, (CHUNK, CHUNK), 1)
    lower = row >= col
    strict = row > col
    eye = jnp.where(row == col, 1.0, 0.0).astype(F32)
    zpad = jnp.zeros((LANES - CHUNK, LANES), F32)

    ba = ba_ref[0]
    beta_all = _sigmoid(ba)
    g_all = -jnp.exp(alog_ref[...]) * _softplus(ba + dtb_ref[...])
    g_cum = _segmented_cumsum_rows(g_all, CHUNK)
    g_cum_t = [jnp.transpose(g_cum[pr * LANES:(pr + 1) * LANES]) for pr in range(_PREP_CHUNKS // 2)]

    inst = [(ci, h) for ci in range(_PREP_CHUNKS) for h in range(DN_HEADS)]
    st = []
    for ci, h in inst:
        rs = slice(ci * CHUNK, (ci + 1) * CHUNK)
        hs = slice(h * DN_DK, (h + 1) * DN_DK)
        q = q_ref[0, rs, hs]
        k = k_ref[0, rs, hs]
        beta = beta_all[rs, h:h + 1]
        gc = g_cum[rs, DN_HEADS + h:DN_HEADS + h + 1]
        gr = g_cum_t[ci // 2][DN_HEADS + h:DN_HEADS + h + 1, (ci % 2) * CHUNK:(ci % 2 + 1) * CHUNK]
        decay = jnp.exp(jnp.where(lower, gc - gr, NEG_INF))
        kb = k * beta
        aq = _dot_nt(jnp.concatenate([kb, q], axis=0).astype(BF16), k.astype(BF16))
        st.append(dict(rs=rs, hs=hs, beta=beta, gc=gc, kb=kb, aq=aq, decay=decay))

    for s in st:
        s["p"] = -jnp.where(strict, s["aq"][:CHUNK] * s["decay"], 0.0)
        s["tinv"] = eye + s["p"]
    n = 1
    while 2 * n < CHUNK:
        for s in st:
            p16 = s["p"].astype(BF16)
            s["p"] = _dot(p16, p16)
        for s in st:
            s["tinv"] = s["tinv"] + _dot(s["tinv"].astype(BF16), s["p"].astype(BF16))
        n *= 2

    for s in st:
        exp_g = jnp.exp(s["gc"])
        v = v_ref[0, s["rs"], s["hs"]]
        rhs = jnp.concatenate([v * s["beta"], s["kb"] * exp_g], axis=1)
        s["sol"] = _dot(s["tinv"].astype(BF16), rhs.astype(BF16))
        s["exp_g"] = exp_g

    for (ci, h), s in zip(inst, st):
        q = q_ref[0, s["rs"], s["hs"]]
        k = k_ref[0, s["rs"], s["hs"]]
        g_last = s["gc"][CHUNK - 1:CHUNK, :]
        k_dec = k * jnp.exp(g_last - s["gc"])
        kdt = jnp.transpose(jnp.concatenate([k_dec, zpad], axis=0))[:, :CHUNK]
        wq_ref[0, ci, h * 2 * CHUNK:(h + 1) * 2 * CHUNK, :] = jnp.concatenate(
            [s["sol"][:, DN_DV:], q * s["exp_g"]], axis=0).astype(BF16)
        u_ref[0, ci, h * CHUNK:(h + 1) * CHUNK, :] = s["sol"][:, :DN_DV]
        qk_ref[0, ci, h * CHUNK:(h + 1) * CHUNK, :] = (s["aq"][CHUNK:] * s["decay"]).astype(BF16)
        kdt_ref[0, ci, h * DN_DK:(h + 1) * DN_DK, :] = kdt.astype(BF16)
        gl_ref[0, ci, h * SUBLANES:(h + 1) * SUBLANES, :] = jnp.broadcast_to(
            jnp.exp(g_last), (SUBLANES, LANES))


def _delta_prep(qkvz, ba, alog, dtb):
    batch, seq, _ = qkvz.shape
    n = seq // CHUNK
    rows = _PREP_CHUNKS * CHUNK
    hw = DN_HEADS * DN_DK

    def col_spec(cb):
        return pl.BlockSpec((1, rows, hw), lambda b, i: (b, i, cb))

    def out_spec(r, c):
        return pl.BlockSpec((1, _PREP_CHUNKS, r, c), lambda b, i: (b, i, 0, 0))

    return pl.pallas_call(
        _delta_prep_kernel,
        grid=(batch, n // _PREP_CHUNKS),
        in_specs=[col_spec(0), col_spec(1), col_spec(2),
                  pl.BlockSpec((1, rows, LANES), lambda b, i: (b, i, 0)),
                  pl.BlockSpec((1, LANES), lambda b, i: (0, 0)),
                  pl.BlockSpec((1, LANES), lambda b, i: (0, 0))],
        out_specs=[out_spec(DN_HEADS * 2 * CHUNK, DN_DK), out_spec(DN_HEADS * CHUNK, DN_DV),
                   out_spec(DN_HEADS * CHUNK, CHUNK), out_spec(DN_HEADS * DN_DK, CHUNK),
                   out_spec(DN_HEADS * SUBLANES, LANES)],
        out_shape=[jax.ShapeDtypeStruct((batch, n, DN_HEADS * 2 * CHUNK, DN_DK), BF16),
                   jax.ShapeDtypeStruct((batch, n, DN_HEADS * CHUNK, DN_DV), F32),
                   jax.ShapeDtypeStruct((batch, n, DN_HEADS * CHUNK, CHUNK), BF16),
                   jax.ShapeDtypeStruct((batch, n, DN_HEADS * DN_DK, CHUNK), BF16),
                   jax.ShapeDtypeStruct((batch, n, DN_HEADS * SUBLANES, LANES), F32)],
        compiler_params=_params("parallel", "parallel"),
        name="delta_prep",
    )(qkvz, qkvz, qkvz, ba, alog, dtb)


def _delta_scan_kernel(batch, wq_ref, u_ref, qk_ref, kdt_ref, gl_ref, z_ref, nw_ref, o_ref, state_ref):
    c = pl.program_id(0)

    @pl.when(c == 0)
    def _():
        state_ref[...] = jnp.zeros_like(state_ref)

    chains = [(b, h) for b in range(batch) for h in range(DN_HEADS)]
    r = [_dot(wq_ref[b, 0, h * 2 * CHUNK:(h + 1) * 2 * CHUNK, :],
              state_ref[b * DN_HEADS + h].astype(BF16)) for b, h in chains]
    v_new = [(u_ref[b, 0, h * CHUNK:(h + 1) * CHUNK, :] - r[i][:CHUNK]).astype(BF16)
             for i, (b, h) in enumerate(chains)]
    kv = [_dot(kdt_ref[b, 0, h * DN_DK:(h + 1) * DN_DK, :], v_new[i]) for i, (b, h) in enumerate(chains)]
    qv = [_dot(qk_ref[b, 0, h * CHUNK:(h + 1) * CHUNK, :], v_new[i]) for i, (b, h) in enumerate(chains)]
    for i, (b, h) in enumerate(chains):
        state_ref[b * DN_HEADS + h] = (
            state_ref[b * DN_HEADS + h] * gl_ref[b, 0, h * SUBLANES:h * SUBLANES + 1, :] + kv[i])
        o = r[i][CHUNK:] + qv[i]
        z = z_ref[b, :, h * DN_DV:(h + 1) * DN_DV]
        o = o * lax.rsqrt(jnp.mean(o * o, axis=-1, keepdims=True) + NORM_EPS)
        o = o * nw_ref[...] * (z * _sigmoid(z))
        o_ref[b, :, h * DN_DV:(h + 1) * DN_DV] = o.astype(o_ref.dtype)


def _delta_scan(wq, u, qk, kdt, gl, qkvz, nw):
    batch, seq, _ = qkvz.shape
    hw = DN_HEADS * DN_DV

    def step_spec(a):
        return pl.BlockSpec((batch, 1) + a.shape[2:], lambda c: (0, c, 0, 0))

    return pl.pallas_call(
        functools.partial(_delta_scan_kernel, batch),
        grid=(seq // CHUNK,),
        in_specs=[step_spec(wq), step_spec(u), step_spec(qk), step_spec(kdt), step_spec(gl),
                  pl.BlockSpec((batch, CHUNK, hw), lambda c: (0, c, 3)),
                  pl.BlockSpec((1, DN_DV), lambda c: (0, 0))],
        out_specs=pl.BlockSpec((batch, CHUNK, hw), lambda c: (0, c, 0)),
        out_shape=jax.ShapeDtypeStruct((batch, seq, hw), BF16),
        scratch_shapes=[pltpu.VMEM((batch * DN_HEADS, DN_DK, DN_DV), F32)],
        compiler_params=_params("arbitrary"),
        name="delta_scan",
    )(wq, u, qk, kdt, gl, qkvz, nw)


def _lru_kernel(tiles_per_seq, xg_ref, wg_ref, gb_ref, lam_ref, o_ref, h_ref):
    i = pl.program_id(0)
    rows = xg_ref.shape[0]
    xc = xg_ref[:, :LRU_W]
    gate = xg_ref[:, LRU_W:]
    gates = _dot(xc.astype(BF16), wg_ref[...]) + gb_ref[...]
    r = _sigmoid(gates[:, :LRU_W])
    ig = _sigmoid(gates[:, LRU_W:])
    log_a = -LRU_C * r * _softplus(-lam_ref[...])
    a = jnp.exp(log_a)
    u = xc * ig * jnp.sqrt(-jnp.tanh(log_a) * (1.0 + a * a))
    row = lax.broadcasted_iota(jnp.int32, a.shape, 0)
    d = 1
    while d < rows:
        a_sh = jnp.where(row >= d, pltpu.roll(a, d, axis=0), 1.0)
        u_sh = jnp.where(row >= d, pltpu.roll(u, d, axis=0), 0.0)
        u = a * u_sh + u
        a = a * a_sh
        d *= 2
    first = (i % tiles_per_seq) == 0
    h_prev = jnp.where(first, 0.0, h_ref[...])
    h = u + a * h_prev
    h_ref[...] = h[rows - 1:rows, :]
    o_ref[...] = (h * gate).astype(o_ref.dtype)


def _lru(xg, wg, gb, lam, *, seq, tm=256):
    t = xg.shape[0]
    return pl.pallas_call(
        functools.partial(_lru_kernel, seq // tm),
        grid=(t // tm,),
        in_specs=[
            pl.BlockSpec((tm, 2 * LRU_W), lambda i: (i, 0)),
            pl.BlockSpec((LRU_W, 2 * LRU_W), lambda i: (0, 0)),
            pl.BlockSpec((1, 2 * LRU_W), lambda i: (0, 0)),
            pl.BlockSpec((1, LRU_W), lambda i: (0, 0)),
        ],
        out_specs=pl.BlockSpec((tm, LRU_W), lambda i: (i, 0)),
        out_shape=jax.ShapeDtypeStruct((t, LRU_W), BF16),
        scratch_shapes=[pltpu.VMEM((1, LRU_W), F32)],
        compiler_params=_params("arbitrary"),
        name="rg_lru",
    )(xg, wg, gb, lam)


_QB = 4 * CHUNK
_KB = 3
_CHUNK_SHIFT = CHUNK.bit_length() - 1


def _band_kernel(q_ref, k0_ref, k1_ref, k2_ref, v0_ref, v1_ref, v2_ref, rel_ref, o_ref, bias_ref):
    i = pl.program_id(1)

    @pl.when((pl.program_id(0) == 0) & (i == 0))
    def _():
        qpos = lax.broadcasted_iota(jnp.int32, (_QB, _QB), 0)
        kcol = lax.broadcasted_iota(jnp.int32, (_QB, _QB), 1)
        for m in range(_KB):
            chunk_off = ((kcol + (m - (_KB - 1)) * _QB) >> _CHUNK_SHIFT) - (qpos >> _CHUNK_SHIFT)
            valid = (chunk_off <= 0) & (chunk_off >= -LA_PAST)
            for h in range(LA_HEADS):
                ev = jnp.broadcast_to(rel_ref[h, m:m + 1, :], (_QB, 2 * _QB))
                toeplitz = pltpu.roll(ev, _QB, axis=1, stride=1, stride_axis=0)[:, :_QB]
                bias_ref[h, m] = jnp.where(valid, toeplitz, NEG_INF)

    k_refs = (k0_ref, k1_ref, k2_ref)
    v_refs = (v0_ref, v1_ref, v2_ref)
    pen = [jnp.where(i >= _KB - 1 - m, 0.0, NEG_INF).astype(F32) for m in range(_KB)]

    def scores(h):
        sl = slice(h * LA_HD, (h + 1) * LA_HD)
        q = q_ref[0, :, sl]
        return [_dot_nt(q, k_refs[m][0, :, sl]) for m in range(_KB)]

    s_next = scores(0)
    for h in range(LA_HEADS):
        sl = slice(h * LA_HD, (h + 1) * LA_HD)
        s = s_next
        if h + 1 < LA_HEADS:
            s_next = scores(h + 1)
        s = [s[m] + bias_ref[h, m] + pen[m] for m in range(_KB)]
        mx = s[0].max(axis=-1, keepdims=True)
        for m in range(1, _KB):
            mx = jnp.maximum(mx, s[m].max(axis=-1, keepdims=True))
        p = [jnp.exp(sm - mx) for sm in s]
        den = p[0].sum(axis=-1, keepdims=True)
        for m in range(1, _KB):
            den = den + p[m].sum(axis=-1, keepdims=True)
        acc = _dot(p[0].astype(BF16), v_refs[0][0, :, sl])
        for m in range(1, _KB):
            acc = acc + _dot(p[m].astype(BF16), v_refs[m][0, :, sl])
        o_ref[0, :, sl] = (acc / den).astype(o_ref.dtype)


def _band_attention(qkv, rel_rows, *, batch, seq):
    hw = LA_HEADS * LA_HD
    nblk = seq // _QB

    def kv_spec(col, m):
        return pl.BlockSpec((1, _QB, hw), lambda b, i: (b, jnp.maximum(i - (_KB - 1 - m), 0), col))

    return pl.pallas_call(
        _band_kernel,
        grid=(batch, nblk),
        in_specs=[pl.BlockSpec((1, _QB, hw), lambda b, i: (b, i, 0))]
        + [kv_spec(1, m) for m in range(_KB)]
        + [kv_spec(2, m) for m in range(_KB)]
        + [pl.BlockSpec((LA_HEADS, _KB, 2 * _QB), lambda b, i: (0, 0, 0))],
        out_specs=pl.BlockSpec((1, _QB, hw), lambda b, i: (b, i, 0)),
        out_shape=jax.ShapeDtypeStruct((batch, seq, hw), BF16),
        scratch_shapes=[pltpu.VMEM((LA_HEADS, _KB, _QB, _QB), F32)],
        compiler_params=_params("arbitrary", "arbitrary"),
        name="band_attention",
    )(qkv, qkv, qkv, qkv, qkv, qkv, qkv, rel_rows)


def _band_rel_rows(rel_table):
    t = rel_table.astype(F32)
    lo = _KB * _QB - REL_CLIP
    hi = _QB - 1 - REL_CLIP
    full = jnp.concatenate([jnp.broadcast_to(t[:, :1], (t.shape[0], lo)), t,
                            jnp.broadcast_to(t[:, -1:], (t.shape[0], hi))], axis=1)
    return jnp.stack([full[:, m * _QB:(m + 2) * _QB] for m in range(_KB)], axis=1)


def _mix_out_kernel(ya_ref, yb_ref, yc_ref, gt_ref, x_ref, wb_ref, wo_ref, g_ref, b_ref, o_ref, ob_ref):
    ys = (ya_ref, yb_ref, yc_ref)
    merged = None
    for r in range(N_BRANCH):
        up = _dot(ys[r][...], wb_ref[r])
        term = gt_ref[:, r * D_MODEL:(r + 1) * D_MODEL].astype(F32) * up
        merged = term if merged is None else merged + term
    y = _dot(merged.astype(BF16), wo_ref[...])
    out = _layer_norm(ALPHA * x_ref[...] + y, g_ref[...], b_ref[...])
    o_ref[...] = out
    ob_ref[...] = out.astype(BF16)


def _mix_out(ya, yb, yc, gates, x, wb, wo, g, b, *, tm=512):
    t, d = x.shape
    return pl.pallas_call(
        _mix_out_kernel,
        grid=(t // tm,),
        in_specs=[
            pl.BlockSpec((tm, BR_W), lambda i: (i, 0)),
            pl.BlockSpec((tm, BR_W), lambda i: (i, 0)),
            pl.BlockSpec((tm, BR_W), lambda i: (i, 0)),
            pl.BlockSpec((tm, N_BRANCH * d), lambda i: (i, 0)),
            pl.BlockSpec((tm, d), lambda i: (i, 0)),
            pl.BlockSpec((N_BRANCH, BR_W, d), lambda i: (0, 0, 0)),
            pl.BlockSpec((d, d), lambda i: (0, 0)),
            pl.BlockSpec((1, d), lambda i: (0, 0)),
            pl.BlockSpec((1, d), lambda i: (0, 0)),
        ],
        out_specs=[pl.BlockSpec((tm, d), lambda i: (i, 0)), pl.BlockSpec((tm, d), lambda i: (i, 0))],
        out_shape=[jax.ShapeDtypeStruct((t, d), F32), jax.ShapeDtypeStruct((t, d), BF16)],
        compiler_params=_params("parallel"),
        name="mix_out",
    )(ya, yb, yc, gates, x, wb, wo, g, b)


def _xattn_kernel(xb_ref, x_ref, kv_ref, wq_ref, wo_ref, g_ref, b_ref, o_ref, ob_ref):
    q = (_dot(xb_ref[0], wq_ref[...]) * np.float32(XA_HD ** -0.5)).astype(BF16)
    outs = []

    def scores(h):
        sl = slice(h * XA_HD, (h + 1) * XA_HD)
        return _dot_nt(q[:, sl], kv_ref[0, :, sl])

    s_next = scores(0)
    for h in range(XA_HEADS):
        v = kv_ref[0, :, D_MODEL + h * XA_HD:D_MODEL + (h + 1) * XA_HD]
        s = s_next
        if h + 1 < XA_HEADS:
            s_next = scores(h + 1)
        p = jnp.exp(s - s.max(axis=-1, keepdims=True))
        den = p.sum(axis=-1, keepdims=True)
        outs.append((_dot(p.astype(BF16), v) / den).astype(BF16))
    o = jnp.concatenate(outs, axis=1)
    y = _dot(o, wo_ref[...])
    out = _layer_norm(ALPHA * x_ref[0] + y, g_ref[...], b_ref[...])
    o_ref[0] = out
    ob_ref[0] = out.astype(BF16)


def _xattn(xb, x, kv, wq, wo, g, b, *, tm=512):
    batch, seq, d = x.shape
    return pl.pallas_call(
        _xattn_kernel,
        grid=(batch, seq // tm),
        in_specs=[
            pl.BlockSpec((1, tm, d), lambda bi, i: (bi, i, 0)),
            pl.BlockSpec((1, tm, d), lambda bi, i: (bi, i, 0)),
            pl.BlockSpec((1, MEM_LEN, 2 * d), lambda bi, i: (bi, 0, 0)),
            pl.BlockSpec((d, d), lambda bi, i: (0, 0)),
            pl.BlockSpec((d, d), lambda bi, i: (0, 0)),
            pl.BlockSpec((1, d), lambda bi, i: (0, 0)),
            pl.BlockSpec((1, d), lambda bi, i: (0, 0)),
        ],
        out_specs=[
            pl.BlockSpec((1, tm, d), lambda bi, i: (bi, i, 0)),
            pl.BlockSpec((1, tm, d), lambda bi, i: (bi, i, 0)),
        ],
        out_shape=[jax.ShapeDtypeStruct((batch, seq, d), F32), jax.ShapeDtypeStruct((batch, seq, d), BF16)],
        compiler_params=_params("parallel", "parallel"),
        name="mem_xattn",
    )(xb, x, kv, wq, wo, g, b)


def _pad_lanes(v, offset):
    out = jnp.zeros((1, LANES), F32)
    return out.at[0, offset:offset + v.shape[0]].set(v.astype(F32))


def _layer(x, mem_b, l, p):
    batch, seq, d = x.shape
    t = batch * seq
    row = lambda v: v.reshape(1, -1).astype(F32)

    x, xb = _ffn(x.reshape(t, d), p["ffn_w12"][l, 0].astype(BF16), p["ffn_w3"][l, 0].astype(BF16),
                 row(p["ln_g"][l, 0]), row(p["ln_b"][l, 0]))

    w_in = p["mix_w_in"][l]
    qkvz = _proj_qkvz(xb, w_in[:, _O_DQ:_O_DB].astype(BF16), p["dn_conv_w"][l].astype(F32), seq=seq)
    w_ba = jnp.zeros((d, LANES), F32).at[:, :2 * DN_HEADS].set(w_in[:, _O_DB:_O_LX]).astype(BF16)
    ba = _proj_plain(xb, w_ba, tn=LANES, out_dtype=F32, name="proj_ba")
    qkvz = qkvz.reshape(batch, seq, -1)
    factors = _delta_prep(qkvz, ba.reshape(batch, seq, LANES),
                          _pad_lanes(p["dn_a_log"][l], DN_HEADS), _pad_lanes(p["dn_dt_bias"][l], DN_HEADS))
    ya = _delta_scan(*factors, qkvz, row(p["dn_norm_w"][l]))

    xg = _proj_lru(xb, w_in[:, _O_LX:_O_AQ].astype(BF16), p["lru_conv_w"][l].astype(F32),
                   row(p["lru_conv_b"][l]), seq=seq)
    gw = p["lru_gate_w"][l].astype(F32)
    eye = jnp.eye(LRU_BLOCKS, dtype=F32)
    wg = jnp.einsum("gnde,nm->gndme", gw, eye).reshape(2, LRU_W, LRU_W)
    wg = jnp.concatenate([wg[0], wg[1]], axis=1).astype(BF16)
    yb = _lru(xg, wg, p["lru_gate_b"][l].reshape(1, -1).astype(F32), row(p["lru_lambda"][l]), seq=seq)

    w_att = w_in[:, _O_AQ:_O_GL]
    w_att = jnp.concatenate([w_att[:, :LA_HEADS * LA_HD] * np.float32(LA_HD ** -0.5),
                             w_att[:, LA_HEADS * LA_HD:]], axis=1).astype(BF16)
    aqkv = _proj_plain(xb, w_att, tn=LA_HEADS * LA_HD, out_dtype=BF16, name="proj_att")
    yc = _band_attention(aqkv.reshape(batch, seq, -1), _band_rel_rows(p["la_rel_bias"][l]),
                         batch=batch, seq=seq)

    gates = _proj_plain(xb, w_in[:, _O_GL:].astype(BF16), tn=D_MODEL, out_dtype=BF16, act="sigmoid",
                        name="proj_gates")
    x, xb = _mix_out(ya.reshape(t, -1), yb, yc.reshape(t, -1), gates, x,
                     p["w_branch"][l].astype(BF16), p["mix_w_out"][l].astype(BF16),
                     row(p["ln_g"][l, 1]), row(p["ln_b"][l, 1]))

    kv = _proj_plain(mem_b, p["xa_wkv"][l].astype(BF16), tn=D_MODEL, out_dtype=BF16, name="proj_kv")
    x, xb = _xattn(xb.reshape(batch, seq, d), x.reshape(batch, seq, d), kv.reshape(batch, MEM_LEN, 2 * d),
                   p["xa_wq"][l].astype(BF16), p["xa_wo"][l].astype(BF16),
                   row(p["ln_g"][l, 2]), row(p["ln_b"][l, 2]))

    x, _ = _ffn(x.reshape(t, d), p["ffn_w12"][l, 1].astype(BF16), p["ffn_w3"][l, 1].astype(BF16),
                row(p["ln_g"][l, 3]), row(p["ln_b"][l, 3]))
    return x.reshape(batch, seq, d)


def kernel(x, mem, ln_g, ln_b, ffn_w12, ffn_w3, mix_w_in, dn_conv_w, dn_a_log, dn_dt_bias, dn_norm_w,
           lru_conv_w, lru_conv_b, lru_gate_w, lru_gate_b, lru_lambda, la_rel_bias, w_branch, mix_w_out,
           xa_wq, xa_wkv, xa_wo):
    p = dict(ln_g=ln_g, ln_b=ln_b, ffn_w12=ffn_w12, ffn_w3=ffn_w3, mix_w_in=mix_w_in, dn_conv_w=dn_conv_w,
             dn_a_log=dn_a_log, dn_dt_bias=dn_dt_bias, dn_norm_w=dn_norm_w, lru_conv_w=lru_conv_w,
             lru_conv_b=lru_conv_b, lru_gate_w=lru_gate_w, lru_gate_b=lru_gate_b, lru_lambda=lru_lambda,
             la_rel_bias=la_rel_bias, w_branch=w_branch, mix_w_out=mix_w_out, xa_wq=xa_wq, xa_wkv=xa_wkv,
             xa_wo=xa_wo)
    batch = x.shape[0]
    mem_b = mem.reshape(batch * MEM_LEN, D_MODEL).astype(BF16)
    x = x.astype(F32)
    for l in range(DEPTH):
        x = _layer(x, mem_b, l, p)
    return x
```

```python
import functools

import numpy as np
import jax
import jax.numpy as jnp
from jax import lax
from jax.experimental import pallas as pl
from jax.experimental.pallas import tpu as pltpu

F32 = jnp.float32
BF16 = jnp.bfloat16

D_MODEL = 1024
DEPTH = 2
CHUNK = 64
CONV_W = 4
BR_W = 512
N_BRANCH = 3
DN_HEADS = 4
DN_DK = 128
DN_DV = 128
LRU_W = BR_W
LRU_BLOCKS = 8
LRU_BLK = LRU_W // LRU_BLOCKS
LRU_C = 8.0
LA_HEADS = 8
LA_HD = 64
LA_PAST = 8
REL_CLIP = 128
MEM_LEN = 256
XA_HEADS = 4
XA_HD = D_MODEL // XA_HEADS
D_FF = 2816
ALPHA = (2 * DEPTH) ** 0.25
LN_EPS = 1e-5
NORM_EPS = 1e-6
NEG_INF = -1e30

LANES = 128
SUBLANES = 8
VMEM_LIMIT = 56 * 1024 * 1024

_O_DZ = 3 * DN_HEADS * DN_DK
_O_DB = _O_DZ + DN_HEADS * DN_DV
_O_LX = _O_DB + 2 * DN_HEADS
_O_AQ = _O_LX + 2 * LRU_W
_O_GL = _O_AQ + 3 * LA_HEADS * LA_HD


def _params(*sem):
    return pltpu.CompilerParams(dimension_semantics=sem, vmem_limit_bytes=VMEM_LIMIT)


def _dot(a, b):
    return jnp.dot(a, b, preferred_element_type=F32)


def _dot_nt(a, b):
    return lax.dot_general(a, b, (((1,), (1,)), ((), ())), preferred_element_type=F32)


def _sigmoid(x):
    return 1.0 / (1.0 + jnp.exp(-x))


def _softplus(x):
    return jnp.maximum(x, 0.0) + jnp.log1p(jnp.exp(-jnp.abs(x)))


def _gelu_tanh(x):
    c = np.float32(np.sqrt(2.0 / np.pi))
    return 0.5 * x * (1.0 + jnp.tanh(c * (x + np.float32(0.044715) * (x * x * x))))


def _layer_norm(y, g, b):
    mu = jnp.mean(y, axis=-1, keepdims=True)
    d = y - mu
    var = jnp.mean(d * d, axis=-1, keepdims=True)
    return d * lax.rsqrt(var + LN_EPS) * g + b


def _resident(shape):
    nd = len(shape)
    return pl.BlockSpec(shape, lambda *_: (0,) * nd, pipeline_mode=pl.Buffered(1))


_FF_CHUNK = 256


def _ffn_kernel(x_ref, w12_ref, w3_ref, g_ref, b_ref, o_ref, ob_ref):
    x = x_ref[...]
    xb = x.astype(BF16)
    n = D_FF // _FF_CHUNK

    def gate_up(c):
        lo = c * _FF_CHUNK
        return (_dot(xb, w12_ref[:, lo:lo + _FF_CHUNK]),
                _dot(xb, w12_ref[:, D_FF + lo:D_FF + lo + _FF_CHUNK]))

    nxt = gate_up(0)
    acc = None
    for c in range(n):
        g, u = nxt
        if c + 1 < n:
            nxt = gate_up(c + 1)
        h = (g * _sigmoid(g) * u).astype(BF16)
        part = _dot(h, w3_ref[c * _FF_CHUNK:(c + 1) * _FF_CHUNK, :])
        acc = part if acc is None else acc + part
    out = _layer_norm(ALPHA * x + 0.5 * acc, g_ref[...], b_ref[...])
    o_ref[...] = out
    ob_ref[...] = out.astype(BF16)


def _ffn(x, w12, w3, g, b, *, tm=512):
    t, d = x.shape
    return pl.pallas_call(
        _ffn_kernel,
        grid=(t // tm,),
        in_specs=[
            pl.BlockSpec((tm, d), lambda i: (i, 0)),
            _resident(w12.shape),
            _resident(w3.shape),
            _resident((1, d)),
            _resident((1, d)),
        ],
        out_specs=[
            pl.BlockSpec((tm, d), lambda i: (i, 0)),
            pl.BlockSpec((tm, d), lambda i: (i, 0)),
        ],
        out_shape=[jax.ShapeDtypeStruct((t, d), F32), jax.ShapeDtypeStruct((t, d), BF16)],
        compiler_params=_params("parallel"),
        name="ffn_ln",
    )(x, w12, w3, g, b)


_HW = DN_HEADS * DN_DK
_C_QKV = 0
_C_Z = 3 * _HW
_C_BA = _C_Z + _HW
_C_LX = _C_BA + LANES
_C_LG = _C_LX + LRU_W
_C_ATT = _C_LG + LRU_W
_C_GL = _C_ATT + 3 * LA_HEADS * LA_HD
_N_TAILS = 4


def _causal_conv(y, tail, cw):
    row8 = lax.broadcasted_iota(jnp.int32, (SUBLANES, y.shape[1]), 0)
    acc = y * cw[CONV_W - 1:CONV_W, :]
    fix = jnp.zeros((SUBLANES, y.shape[1]), F32)
    for k in range(1, CONV_W):
        wk = cw[CONV_W - 1 - k:CONV_W - k, :]
        sh = pltpu.roll(y, k, axis=0)
        acc = acc + sh * wk
        prev = pltpu.roll(tail, k, axis=0)
        fix = fix + jnp.where(row8 < k, (prev - sh[:SUBLANES]) * wk, 0.0)
    return jnp.concatenate([acc[:SUBLANES] + fix, acc[SUBLANES:]], axis=0)


def _mixer_in_kernel(tiles_per_seq, x_ref, w_ref, dcw_ref, lcw_ref, lcb_ref,
                     qkvz_ref, ba_ref, xg_ref, att_ref, gt_ref, tail_ref):
    xb = x_ref[...]
    first = (pl.program_id(0) % tiles_per_seq) == 0
    rows = xb.shape[0]

    def proj(c0, width=_HW):
        return _dot(xb, w_ref[:, c0:c0 + width])

    def conv(y, slot, cw):
        tail = jnp.where(first, 0.0, tail_ref[slot])
        tail_ref[slot] = y[rows - SUBLANES:]
        return _causal_conv(y, tail, cw)

    for j in range(3):
        c = conv(proj(_C_QKV + j * _HW), j, dcw_ref[:, j * _HW:(j + 1) * _HW])
        c = c * _sigmoid(c)
        if j < 2:
            scale = np.float32(DN_DK ** -0.5 if j == 0 else 1.0)
            parts = []
            for h in range(DN_HEADS):
                ch = c[:, h * DN_DK:(h + 1) * DN_DK]
                parts.append(ch * (lax.rsqrt(jnp.sum(ch * ch, axis=-1, keepdims=True) + NORM_EPS) * scale))
            c = jnp.concatenate(parts, axis=1)
        qkvz_ref[:, j * _HW:(j + 1) * _HW] = c.astype(BF16)
    qkvz_ref[:, 3 * _HW:] = proj(_C_Z).astype(BF16)
    ba_ref[...] = proj(_C_BA, LANES)
    xg_ref[:, :LRU_W] = (conv(proj(_C_LX), 3, lcw_ref[...]) + lcb_ref[...]).astype(BF16)
    xg_ref[:, LRU_W:] = _gelu_tanh(proj(_C_LG)).astype(BF16)
    for j in range(3):
        att_ref[:, j * _HW:(j + 1) * _HW] = proj(_C_ATT + j * _HW).astype(BF16)
    for j in range(N_BRANCH * D_MODEL // _HW):
        gt_ref[:, j * _HW:(j + 1) * _HW] = _sigmoid(proj(_C_GL + j * _HW)).astype(BF16)


def _mixer_in(xb, w, dcw, lcw, lcb, *, seq, tm=512):
    t, d = xb.shape

    def tile(width):
        return pl.BlockSpec((tm, width), lambda i: (i, 0))

    widths = (4 * _HW, LANES, 2 * LRU_W, 3 * _HW, N_BRANCH * D_MODEL)
    dtypes = (BF16, F32, BF16, BF16, BF16)
    return pl.pallas_call(
        functools.partial(_mixer_in_kernel, seq // tm),
        grid=(t // tm,),
        in_specs=[tile(d), _resident(w.shape), _resident(dcw.shape), _resident(lcw.shape),
                  _resident(lcb.shape)],
        out_specs=[tile(wd) for wd in widths],
        out_shape=[jax.ShapeDtypeStruct((t, wd), dt) for wd, dt in zip(widths, dtypes)],
        scratch_shapes=[pltpu.VMEM((_N_TAILS, SUBLANES, _HW), F32)],
        compiler_params=_params("arbitrary"),
        name="mixer_in",
    )(xb, w, dcw, lcw, lcb)


def _proj_plain_kernel(x_ref, w_ref, o_ref):
    o_ref[...] = _dot(x_ref[...], w_ref[...]).astype(o_ref.dtype)


def _proj_plain(xb, w, *, tn, out_dtype, name):
    t, d = xb.shape
    n = w.shape[1]
    return pl.pallas_call(
        _proj_plain_kernel,
        grid=(n // tn,),
        in_specs=[pl.BlockSpec((t, d), lambda j: (0, 0)), pl.BlockSpec((d, tn), lambda j: (0, j))],
        out_specs=pl.BlockSpec((t, tn), lambda j: (0, j)),
        out_shape=jax.ShapeDtypeStruct((t, n), out_dtype),
        compiler_params=_params("parallel"),
        name=name,
    )(xb, w)


_PREP_CHUNKS = 4
_SCAN_CHUNKS = 2


def _segmented_cumsum_rows(x, seg):
    row = lax.broadcasted_iota(jnp.int32, x.shape, 0) & (seg - 1)
    d = 1
    while d < seg:
        x = x + jnp.where(row >= d, pltpu.roll(x, d, axis=0), 0.0)
        d *= 2
    return x


def _delta_prep_kernel(q_ref, k_ref, v_ref, ba_ref, alog_ref, dtb_ref,
                       wq_ref, u_ref, qk_ref, kdt_ref, gl_ref):
    row = lax.broadcasted_iota(jnp.int32, (CHUNK, CHUNK), 0)
    col = lax.broadcasted_iota(jnp.int32, (CHUNK, CHUNK), 1)
    lower = row >= col
    strict = row > col
    eye = jnp.where(row == col, 1.0, 0.0).astype(F32)
    zpad = jnp.zeros((LANES - CHUNK, LANES), F32)

    ba = ba_ref[0]
    beta_all = _sigmoid(ba)
    g_all = -jnp.exp(alog_ref[...]) * _softplus(ba + dtb_ref[...])
    g_cum = _segmented_cumsum_rows(g_all, CHUNK)
    g_cum_t = [jnp.transpose(g_cum[pr * LANES:(pr + 1) * LANES]) for pr in range(_PREP_CHUNKS // 2)]

    inst = [(ci, h) for ci in range(_PREP_CHUNKS) for h in range(DN_HEADS)]
    st = []
    for ci, h in inst:
        rs = slice(ci * CHUNK, (ci + 1) * CHUNK)
        hs = slice(h * DN_DK, (h + 1) * DN_DK)
        q = q_ref[0, rs, hs].astype(F32)
        k = k_ref[0, rs, hs].astype(F32)
        beta = beta_all[rs, h:h + 1]
        gc = g_cum[rs, DN_HEADS + h:DN_HEADS + h + 1]
        gr = g_cum_t[ci // 2][DN_HEADS + h:DN_HEADS + h + 1, (ci % 2) * CHUNK:(ci % 2 + 1) * CHUNK]
        decay = jnp.exp(jnp.where(lower, gc - gr, NEG_INF))
        kb = k * beta
        aq = _dot_nt(jnp.concatenate([kb, q], axis=0).astype(BF16), k.astype(BF16))
        st.append(dict(rs=rs, hs=hs, beta=beta, gc=gc, kb=kb, aq=aq, decay=decay))

    for s in st:
        s["p"] = -jnp.where(strict, s["aq"][:CHUNK] * s["decay"], 0.0)
        s["tinv"] = eye + s["p"]
    n = 1
    while 2 * n < CHUNK:
        for s in st:
            p16 = s["p"].astype(BF16)
            s["p"] = _dot(p16, p16)
        for s in st:
            s["tinv"] = s["tinv"] + _dot(s["tinv"].astype(BF16), s["p"].astype(BF16))
        n *= 2

    for s in st:
        exp_g = jnp.exp(s["gc"])
        v = v_ref[0, s["rs"], s["hs"]].astype(F32)
        rhs = jnp.concatenate([v * s["beta"], s["kb"] * exp_g], axis=1)
        s["sol"] = _dot(s["tinv"].astype(BF16), rhs.astype(BF16))
        s["exp_g"] = exp_g

    for (ci, h), s in zip(inst, st):
        q = q_ref[0, s["rs"], s["hs"]].astype(F32)
        k = k_ref[0, s["rs"], s["hs"]].astype(F32)
        g_last = s["gc"][CHUNK - 1:CHUNK, :]
        k_dec = k * jnp.exp(g_last - s["gc"])
        kdt = jnp.transpose(jnp.concatenate([k_dec, zpad], axis=0))[:, :CHUNK]
        wq_ref[0, ci, h * 2 * CHUNK:(h + 1) * 2 * CHUNK, :] = jnp.concatenate(
            [s["sol"][:, DN_DV:], q * s["exp_g"]], axis=0).astype(BF16)
        u_ref[0, ci, h * CHUNK:(h + 1) * CHUNK, :] = s["sol"][:, :DN_DV]
        qk_ref[0, ci, h * CHUNK:(h + 1) * CHUNK, :] = (s["aq"][CHUNK:] * s["decay"]).astype(BF16)
        kdt_ref[0, ci, h * DN_DK:(h + 1) * DN_DK, :] = kdt.astype(BF16)
        gl_ref[0, ci, h * SUBLANES:(h + 1) * SUBLANES, :] = jnp.broadcast_to(
            jnp.exp(g_last), (SUBLANES, LANES))


def _delta_prep(qkvz, ba, alog, dtb):
    batch, seq, _ = qkvz.shape
    n = seq // CHUNK
    rows = _PREP_CHUNKS * CHUNK
    hw = DN_HEADS * DN_DK

    def col_spec(cb):
        return pl.BlockSpec((1, rows, hw), lambda b, i: (b, i, cb))

    def out_spec(r, c):
        return pl.BlockSpec((1, _PREP_CHUNKS, r, c), lambda b, i: (b, i, 0, 0))

    return pl.pallas_call(
        _delta_prep_kernel,
        grid=(batch, n // _PREP_CHUNKS),
        in_specs=[col_spec(0), col_spec(1), col_spec(2),
                  pl.BlockSpec((1, rows, LANES), lambda b, i: (b, i, 0)),
                  pl.BlockSpec((1, LANES), lambda b, i: (0, 0)),
                  pl.BlockSpec((1, LANES), lambda b, i: (0, 0))],
        out_specs=[out_spec(DN_HEADS * 2 * CHUNK, DN_DK), out_spec(DN_HEADS * CHUNK, DN_DV),
                   out_spec(DN_HEADS * CHUNK, CHUNK), out_spec(DN_HEADS * DN_DK, CHUNK),
                   out_spec(DN_HEADS * SUBLANES, LANES)],
        out_shape=[jax.ShapeDtypeStruct((batch, n, DN_HEADS * 2 * CHUNK, DN_DK), BF16),
                   jax.ShapeDtypeStruct((batch, n, DN_HEADS * CHUNK, DN_DV), F32),
                   jax.ShapeDtypeStruct((batch, n, DN_HEADS * CHUNK, CHUNK), BF16),
                   jax.ShapeDtypeStruct((batch, n, DN_HEADS * DN_DK, CHUNK), BF16),
                   jax.ShapeDtypeStruct((batch, n, DN_HEADS * SUBLANES, LANES), F32)],
        compiler_params=_params("parallel", "parallel"),
        name="delta_prep",
    )(qkvz, qkvz, qkvz, ba, alog, dtb)


def _delta_scan_kernel(batch, wq_ref, u_ref, qk_ref, kdt_ref, gl_ref, z_ref, nw_ref, o_ref, state_ref):
    c = pl.program_id(0)

    @pl.when(c == 0)
    def _():
        state_ref[...] = jnp.zeros_like(state_ref)

    chains = [(b, h) for b in range(batch) for h in range(DN_HEADS)]
    for cc in range(_SCAN_CHUNKS):
        rs = slice(cc * CHUNK, (cc + 1) * CHUNK)
        r = [_dot(wq_ref[b, cc, h * 2 * CHUNK:(h + 1) * 2 * CHUNK, :],
                  state_ref[b * DN_HEADS + h].astype(BF16)) for b, h in chains]
        v_new = [(u_ref[b, cc, h * CHUNK:(h + 1) * CHUNK, :] - r[i][:CHUNK]).astype(BF16)
                 for i, (b, h) in enumerate(chains)]
        kv = [_dot(kdt_ref[b, cc, h * DN_DK:(h + 1) * DN_DK, :], v_new[i]) for i, (b, h) in enumerate(chains)]
        qv = [_dot(qk_ref[b, cc, h * CHUNK:(h + 1) * CHUNK, :], v_new[i]) for i, (b, h) in enumerate(chains)]
        for i, (b, h) in enumerate(chains):
            state_ref[b * DN_HEADS + h] = (
                state_ref[b * DN_HEADS + h] * gl_ref[b, cc, h * SUBLANES:h * SUBLANES + 1, :] + kv[i])
            o = r[i][CHUNK:] + qv[i]
            z = z_ref[b, rs, h * DN_DV:(h + 1) * DN_DV].astype(F32)
            o = o * lax.rsqrt(jnp.mean(o * o, axis=-1, keepdims=True) + NORM_EPS)
            o = o * nw_ref[...] * (z * _sigmoid(z))
            o_ref[b, rs, h * DN_DV:(h + 1) * DN_DV] = o.astype(o_ref.dtype)


def _delta_scan(wq, u, qk, kdt, gl, qkvz, nw):
    batch, seq, _ = qkvz.shape
    hw = DN_HEADS * DN_DV

    rows = _SCAN_CHUNKS * CHUNK

    def step_spec(a):
        return pl.BlockSpec((batch, _SCAN_CHUNKS) + a.shape[2:], lambda c: (0, c, 0, 0))

    return pl.pallas_call(
        functools.partial(_delta_scan_kernel, batch),
        grid=(seq // rows,),
        in_specs=[step_spec(wq), step_spec(u), step_spec(qk), step_spec(kdt), step_spec(gl),
                  pl.BlockSpec((batch, rows, hw), lambda c: (0, c, 3)),
                  pl.BlockSpec((1, DN_DV), lambda c: (0, 0))],
        out_specs=pl.BlockSpec((batch, rows, hw), lambda c: (0, c, 0)),
        out_shape=jax.ShapeDtypeStruct((batch, seq, hw), BF16),
        scratch_shapes=[pltpu.VMEM((batch * DN_HEADS, DN_DK, DN_DV), F32)],
        compiler_params=_params("arbitrary"),
        name="delta_scan",
    )(wq, u, qk, kdt, gl, qkvz, nw)


def _lru_kernel(tiles_per_seq, xg_ref, wg_ref, gb_ref, lam_ref, o_ref, h_ref):
    i = pl.program_id(0)
    rows = xg_ref.shape[0]
    xc = xg_ref[:, :LRU_W].astype(F32)
    gate = xg_ref[:, LRU_W:].astype(F32)
    gates = _dot(xg_ref[:, :LRU_W], wg_ref[...]) + gb_ref[...]
    r = _sigmoid(gates[:, :LRU_W])
    ig = _sigmoid(gates[:, LRU_W:])
    log_a = -LRU_C * r * _softplus(-lam_ref[...])
    a = jnp.exp(log_a)
    u = xc * ig * jnp.sqrt(-jnp.tanh(log_a) * (1.0 + a * a))
    row = lax.broadcasted_iota(jnp.int32, a.shape, 0)
    d = 1
    while d < rows:
        a_sh = jnp.where(row >= d, pltpu.roll(a, d, axis=0), 1.0)
        u_sh = jnp.where(row >= d, pltpu.roll(u, d, axis=0), 0.0)
        u = a * u_sh + u
        a = a * a_sh
        d *= 2
    first = (i % tiles_per_seq) == 0
    h_prev = jnp.where(first, 0.0, h_ref[...])
    h = u + a * h_prev
    h_ref[...] = h[rows - 1:rows, :]
    o_ref[...] = (h * gate).astype(o_ref.dtype)


def _lru(xg, wg, gb, lam, *, seq, tm=256):
    t = xg.shape[0]
    return pl.pallas_call(
        functools.partial(_lru_kernel, seq // tm),
        grid=(t // tm,),
        in_specs=[
            pl.BlockSpec((tm, 2 * LRU_W), lambda i: (i, 0)),
            pl.BlockSpec((LRU_W, 2 * LRU_W), lambda i: (0, 0)),
            pl.BlockSpec((1, 2 * LRU_W), lambda i: (0, 0)),
            pl.BlockSpec((1, LRU_W), lambda i: (0, 0)),
        ],
        out_specs=pl.BlockSpec((tm, LRU_W), lambda i: (i, 0)),
        out_shape=jax.ShapeDtypeStruct((t, LRU_W), BF16),
        scratch_shapes=[pltpu.VMEM((1, LRU_W), F32)],
        compiler_params=_params("arbitrary"),
        name="rg_lru",
    )(xg, wg, gb, lam)


_QB = 4 * CHUNK
_KB = 3
_CHUNK_SHIFT = CHUNK.bit_length() - 1


def _band_kernel(q_ref, k0_ref, k1_ref, k2_ref, v0_ref, v1_ref, v2_ref, rel_ref, o_ref, bias_ref):
    i = pl.program_id(1)

    @pl.when((pl.program_id(0) == 0) & (i == 0))
    def _():
        qpos = lax.broadcasted_iota(jnp.int32, (_QB, _QB), 0)
        kcol = lax.broadcasted_iota(jnp.int32, (_QB, _QB), 1)
        for m in range(_KB):
            chunk_off = ((kcol + (m - (_KB - 1)) * _QB) >> _CHUNK_SHIFT) - (qpos >> _CHUNK_SHIFT)
            valid = (chunk_off <= 0) & (chunk_off >= -LA_PAST)
            for h in range(LA_HEADS):
                ev = jnp.broadcast_to(rel_ref[h, m:m + 1, :], (_QB, 2 * _QB))
                toeplitz = pltpu.roll(ev, _QB, axis=1, stride=1, stride_axis=0)[:, :_QB]
                bias_ref[h, m] = jnp.where(valid, toeplitz, NEG_INF)

    k_refs = (k0_ref, k1_ref, k2_ref)
    v_refs = (v0_ref, v1_ref, v2_ref)
    lane = lax.broadcasted_iota(jnp.int32, (1, LANES), 1)
    half_sel = (lane < LA_HD, lane >= LA_HD)

    def attend(ms):
        def scores(h):
            ps = slice((h // 2) * LANES, (h // 2 + 1) * LANES)
            qh = jnp.where(half_sel[h % 2], q_ref[0, :, ps], jnp.zeros((), BF16))
            return [_dot_nt(qh, k_refs[m][0, :, ps]) for m in ms]

        s_next = scores(0)
        pair_out = None
        for h in range(LA_HEADS):
            ps = slice((h // 2) * LANES, (h // 2 + 1) * LANES)
            sel = half_sel[h % 2]
            s = s_next
            if h + 1 < LA_HEADS:
                s_next = scores(h + 1)
            s = [sm + bias_ref[h, m] for sm, m in zip(s, ms)]
            mx = s[0].max(axis=-1, keepdims=True)
            for sm in s[1:]:
                mx = jnp.maximum(mx, sm.max(axis=-1, keepdims=True))
            acc = None
            for sm, m in zip(s, ms):
                vh = jnp.where(sel, v_refs[m][0, :, ps], jnp.ones((), BF16))
                part = _dot(jnp.exp(sm - mx).astype(BF16), vh)
                acc = part if acc is None else acc + part
            den = pltpu.roll(acc, LA_HD, axis=1)
            out = jnp.where(sel, acc / den, 0.0)
            if h % 2 == 0:
                pair_out = out
            else:
                o_ref[0, :, ps] = (pair_out + out).astype(o_ref.dtype)

    for nvalid in range(1, _KB + 1):
        pl.when(jnp.minimum(i, _KB - 1) == nvalid - 1)(
            functools.partial(attend, tuple(range(_KB - nvalid, _KB))))


def _band_attention(qkv, rel_rows, *, batch, seq):
    hw = LA_HEADS * LA_HD
    nblk = seq // _QB

    def kv_spec(col, m):
        return pl.BlockSpec((1, _QB, hw), lambda b, i: (b, jnp.maximum(i - (_KB - 1 - m), 0), col))

    return pl.pallas_call(
        _band_kernel,
        grid=(batch, nblk),
        in_specs=[pl.BlockSpec((1, _QB, hw), lambda b, i: (b, i, 0))]
        + [kv_spec(1, m) for m in range(_KB)]
        + [kv_spec(2, m) for m in range(_KB)]
        + [pl.BlockSpec((LA_HEADS, _KB, 2 * _QB), lambda b, i: (0, 0, 0))],
        out_specs=pl.BlockSpec((1, _QB, hw), lambda b, i: (b, i, 0)),
        out_shape=jax.ShapeDtypeStruct((batch, seq, hw), BF16),
        scratch_shapes=[pltpu.VMEM((LA_HEADS, _KB, _QB, _QB), F32)],
        compiler_params=_params("arbitrary", "arbitrary"),
        name="band_attention",
    )(qkv, qkv, qkv, qkv, qkv, qkv, qkv, rel_rows)


def _band_rel_rows(rel_table):
    t = rel_table.astype(F32)
    lo = _KB * _QB - REL_CLIP
    hi = _QB - 1 - REL_CLIP
    full = jnp.concatenate([jnp.broadcast_to(t[:, :1], (t.shape[0], lo)), t,
                            jnp.broadcast_to(t[:, -1:], (t.shape[0], hi))], axis=1)
    return jnp.stack([full[:, m * _QB:(m + 2) * _QB] for m in range(_KB)], axis=1)


def _mix_out_kernel(ya_ref, yb_ref, yc_ref, gt_ref, x_ref, wb_ref, wo_ref, g_ref, b_ref, o_ref, ob_ref):
    ys = (ya_ref, yb_ref, yc_ref)
    merged = None
    for r in range(N_BRANCH):
        up = _dot(ys[r][...], wb_ref[r])
        term = gt_ref[:, r * D_MODEL:(r + 1) * D_MODEL].astype(F32) * up
        merged = term if merged is None else merged + term
    y = _dot(merged.astype(BF16), wo_ref[...])
    out = _layer_norm(ALPHA * x_ref[...] + y, g_ref[...], b_ref[...])
    o_ref[...] = out
    ob_ref[...] = out.astype(BF16)


def _mix_out(ya, yb, yc, gates, x, wb, wo, g, b, *, tm=512):
    t, d = x.shape
    return pl.pallas_call(
        _mix_out_kernel,
        grid=(t // tm,),
        in_specs=[
            pl.BlockSpec((tm, BR_W), lambda i: (i, 0)),
            pl.BlockSpec((tm, BR_W), lambda i: (i, 0)),
            pl.BlockSpec((tm, BR_W), lambda i: (i, 0)),
            pl.BlockSpec((tm, N_BRANCH * d), lambda i: (i, 0)),
            pl.BlockSpec((tm, d), lambda i: (i, 0)),
            pl.BlockSpec((N_BRANCH, BR_W, d), lambda i: (0, 0, 0)),
            pl.BlockSpec((d, d), lambda i: (0, 0)),
            pl.BlockSpec((1, d), lambda i: (0, 0)),
            pl.BlockSpec((1, d), lambda i: (0, 0)),
        ],
        out_specs=[pl.BlockSpec((tm, d), lambda i: (i, 0)), pl.BlockSpec((tm, d), lambda i: (i, 0))],
        out_shape=[jax.ShapeDtypeStruct((t, d), F32), jax.ShapeDtypeStruct((t, d), BF16)],
        compiler_params=_params("parallel"),
        name="mix_out",
    )(ya, yb, yc, gates, x, wb, wo, g, b)


def _xattn_kernel(xb_ref, x_ref, kv_ref, wq_ref, wo_ref, g_ref, b_ref, o_ref, ob_ref):
    q = (_dot(xb_ref[0], wq_ref[...]) * np.float32(XA_HD ** -0.5)).astype(BF16)
    outs = []

    def scores(h):
        sl = slice(h * XA_HD, (h + 1) * XA_HD)
        return _dot_nt(q[:, sl], kv_ref[0, :, sl])

    s_next = scores(0)
    for h in range(XA_HEADS):
        v = kv_ref[0, :, D_MODEL + h * XA_HD:D_MODEL + (h + 1) * XA_HD]
        s = s_next
        if h + 1 < XA_HEADS:
            s_next = scores(h + 1)
        p = jnp.exp(s - s.max(axis=-1, keepdims=True))
        den = p.sum(axis=-1, keepdims=True)
        outs.append((_dot(p.astype(BF16), v) / den).astype(BF16))
    o = jnp.concatenate(outs, axis=1)
    y = _dot(o, wo_ref[...])
    out = _layer_norm(ALPHA * x_ref[0] + y, g_ref[...], b_ref[...])
    o_ref[0] = out
    ob_ref[0] = out.astype(BF16)


def _xattn(xb, x, kv, wq, wo, g, b, *, tm=512):
    batch, seq, d = x.shape
    return pl.pallas_call(
        _xattn_kernel,
        grid=(batch, seq // tm),
        in_specs=[
            pl.BlockSpec((1, tm, d), lambda bi, i: (bi, i, 0)),
            pl.BlockSpec((1, tm, d), lambda bi, i: (bi, i, 0)),
            pl.BlockSpec((1, MEM_LEN, 2 * d), lambda bi, i: (bi, 0, 0)),
            pl.BlockSpec((d, d), lambda bi, i: (0, 0)),
            pl.BlockSpec((d, d), lambda bi, i: (0, 0)),
            pl.BlockSpec((1, d), lambda bi, i: (0, 0)),
            pl.BlockSpec((1, d), lambda bi, i: (0, 0)),
        ],
        out_specs=[
            pl.BlockSpec((1, tm, d), lambda bi, i: (bi, i, 0)),
            pl.BlockSpec((1, tm, d), lambda bi, i: (bi, i, 0)),
        ],
        out_shape=[jax.ShapeDtypeStruct((batch, seq, d), F32), jax.ShapeDtypeStruct((batch, seq, d), BF16)],
        compiler_params=_params("parallel", "parallel"),
        name="mem_xattn",
    )(xb, x, kv, wq, wo, g, b)


def _pad_lanes(v, offset):
    out = jnp.zeros((1, LANES), F32)
    return out.at[0, offset:offset + v.shape[0]].set(v.astype(F32))


def _layer(x, mem_b, l, p):
    batch, seq, d = x.shape
    t = batch * seq
    row = lambda v: v.reshape(1, -1).astype(F32)

    x, xb = _ffn(x.reshape(t, d), p["ffn_w12"][l, 0].astype(BF16), p["ffn_w3"][l, 0].astype(BF16),
                 row(p["ln_g"][l, 0]), row(p["ln_b"][l, 0]))

    w_in = p["mix_w_in"][l]
    att_scale = jnp.concatenate([jnp.full((LA_HEADS * LA_HD,), LA_HD ** -0.5, F32),
                                 jnp.ones((2 * LA_HEADS * LA_HD,), F32)])
    w_packed = jnp.concatenate([
        w_in[:, :_O_DB],
        jnp.pad(w_in[:, _O_DB:_O_LX], ((0, 0), (0, LANES - 2 * DN_HEADS))),
        w_in[:, _O_LX:_O_AQ],
        w_in[:, _O_AQ:_O_GL] * att_scale,
        w_in[:, _O_GL:]], axis=1).astype(BF16)
    qkvz, ba, xg, aqkv, gates = _mixer_in(xb, w_packed, p["dn_conv_w"][l].astype(F32),
                                          p["lru_conv_w"][l].astype(F32), row(p["lru_conv_b"][l]), seq=seq)

    qkvz = qkvz.reshape(batch, seq, -1)
    factors = _delta_prep(qkvz, ba.reshape(batch, seq, LANES),
                          _pad_lanes(p["dn_a_log"][l], DN_HEADS), _pad_lanes(p["dn_dt_bias"][l], DN_HEADS))
    ya = _delta_scan(*factors, qkvz, row(p["dn_norm_w"][l]))

    gw = p["lru_gate_w"][l].astype(F32)
    eye = jnp.eye(LRU_BLOCKS, dtype=F32)
    wg = jnp.einsum("gnde,nm->gndme", gw, eye).reshape(2, LRU_W, LRU_W)
    wg = jnp.concatenate([wg[0], wg[1]], axis=1).astype(BF16)
    yb = _lru(xg, wg, p["lru_gate_b"][l].reshape(1, -1).astype(F32), row(p["lru_lambda"][l]), seq=seq)

    yc = _band_attention(aqkv.reshape(batch, seq, -1), _band_rel_rows(p["la_rel_bias"][l]),
                         batch=batch, seq=seq)

    x, xb = _mix_out(ya.reshape(t, -1), yb, yc.reshape(t, -1), gates, x,
                     p["w_branch"][l].astype(BF16), p["mix_w_out"][l].astype(BF16),
                     row(p["ln_g"][l, 1]), row(p["ln_b"][l, 1]))

    kv = _proj_plain(mem_b, p["xa_wkv"][l].astype(BF16), tn=D_MODEL, out_dtype=BF16, name="proj_kv")
    x, xb = _xattn(xb.reshape(batch, seq, d), x.reshape(batch, seq, d), kv.reshape(batch, MEM_LEN, 2 * d),
                   p["xa_wq"][l].astype(BF16), p["xa_wo"][l].astype(BF16),
                   row(p["ln_g"][l, 2]), row(p["ln_b"][l, 2]))

    x, _ = _ffn(x.reshape(t, d), p["ffn_w12"][l, 1].astype(BF16), p["ffn_w3"][l, 1].astype(BF16),
                row(p["ln_g"][l, 3]), row(p["ln_b"][l, 3]))
    return x.reshape(batch, seq, d)


def kernel(x, mem, ln_g, ln_b, ffn_w12, ffn_w3, mix_w_in, dn_conv_w, dn_a_log, dn_dt_bias, dn_norm_w,
           lru_conv_w, lru_conv_b, lru_gate_w, lru_gate_b, lru_lambda, la_rel_bias, w_branch, mix_w_out,
           xa_wq, xa_wkv, xa_wo):
    p = dict(ln_g=ln_g, ln_b=ln_b, ffn_w12=ffn_w12, ffn_w3=ffn_w3, mix_w_in=mix_w_in, dn_conv_w=dn_conv_w,
             dn_a_log=dn_a_log, dn_dt_bias=dn_dt_bias, dn_norm_w=dn_norm_w, lru_conv_w=lru_conv_w,
             lru_conv_b=lru_conv_b, lru_gate_w=lru_gate_w, lru_gate_b=lru_gate_b, lru_lambda=lru_lambda,
             la_rel_bias=la_rel_bias, w_branch=w_branch, mix_w_out=mix_w_out, xa_wq=xa_wq, xa_wkv=xa_wkv,
             xa_wo=xa_wo)
    batch = x.shape[0]
    mem_b = mem.reshape(batch * MEM_LEN, D_MODEL).astype(BF16)
    x = x.astype(F32)
    for l in range(DEPTH):
        x = _layer(x, mem_b, l, p)
    return x
```

```python
import functools

import numpy as np
import jax
import jax.numpy as jnp
from jax import lax
from jax.experimental import pallas as pl
from jax.experimental.pallas import tpu as pltpu

F32 = jnp.float32
BF16 = jnp.bfloat16

D_MODEL = 1024
DEPTH = 2
CHUNK = 64
CONV_W = 4
BR_W = 512
N_BRANCH = 3
DN_HEADS = 4
DN_DK = 128
DN_DV = 128
LRU_W = BR_W
LRU_BLOCKS = 8
LRU_BLK = LRU_W // LRU_BLOCKS
LRU_C = 8.0
LA_HEADS = 8
LA_HD = 64
LA_PAST = 8
REL_CLIP = 128
MEM_LEN = 256
XA_HEADS = 4
XA_HD = D_MODEL // XA_HEADS
D_FF = 2816
ALPHA = (2 * DEPTH) ** 0.25
LN_EPS = 1e-5
NORM_EPS = 1e-6
NEG_INF = -1e30

LANES = 128
SUBLANES = 8
VMEM_LIMIT = 56 * 1024 * 1024

_O_DZ = 3 * DN_HEADS * DN_DK
_O_DB = _O_DZ + DN_HEADS * DN_DV
_O_LX = _O_DB + 2 * DN_HEADS
_O_AQ = _O_LX + 2 * LRU_W
_O_GL = _O_AQ + 3 * LA_HEADS * LA_HD


def _params(*sem):
    return pltpu.CompilerParams(dimension_semantics=sem, vmem_limit_bytes=VMEM_LIMIT)


def _dot(a, b):
    return jnp.dot(a, b, preferred_element_type=F32)


def _dot_nt(a, b):
    return lax.dot_general(a, b, (((1,), (1,)), ((), ())), preferred_element_type=F32)


def _sigmoid(x):
    return 1.0 / (1.0 + jnp.exp(-x))


def _softplus(x):
    return jnp.maximum(x, 0.0) + jnp.log1p(jnp.exp(-jnp.abs(x)))


def _gelu_tanh(x):
    c = np.float32(np.sqrt(2.0 / np.pi))
    return 0.5 * x * (1.0 + jnp.tanh(c * (x + np.float32(0.044715) * (x * x * x))))


def _layer_norm(y, g, b):
    mu = jnp.mean(y, axis=-1, keepdims=True)
    d = y - mu
    var = jnp.mean(d * d, axis=-1, keepdims=True)
    return d * lax.rsqrt(var + LN_EPS) * g + b


def _pick(arr, *idx):
    shape = (None,) * len(idx) + tuple(arr.shape[len(idx):])
    index = tuple(idx) + (0,) * (arr.ndim - len(idx))
    return pl.BlockSpec(shape, lambda *_: index, pipeline_mode=pl.Buffered(1))


_FF_CHUNK = 256


def _ffn_kernel(x_ref, w12_ref, w3_ref, g_ref, b_ref, o_ref, ob_ref):
    x = x_ref[...]
    xb = x.astype(BF16)
    n = D_FF // _FF_CHUNK

    def gate_up(c):
        lo = c * _FF_CHUNK
        return (_dot(xb, w12_ref[:, lo:lo + _FF_CHUNK]),
                _dot(xb, w12_ref[:, D_FF + lo:D_FF + lo + _FF_CHUNK]))

    nxt = gate_up(0)
    acc = None
    for c in range(n):
        g, u = nxt
        if c + 1 < n:
            nxt = gate_up(c + 1)
        h = (g * _sigmoid(g) * u).astype(BF16)
        part = _dot(h, w3_ref[c * _FF_CHUNK:(c + 1) * _FF_CHUNK, :])
        acc = part if acc is None else acc + part
    out = _layer_norm(ALPHA * x + 0.5 * acc, g_ref[...], b_ref[...])
    o_ref[...] = out
    ob_ref[...] = out.astype(BF16)


def _ffn(x, w12, w3, ln_g, ln_b, sel, *, tm=512):
    t, d = x.shape
    l, k = sel
    return pl.pallas_call(
        _ffn_kernel,
        grid=(t // tm,),
        in_specs=[
            pl.BlockSpec((tm, d), lambda i: (i, 0)),
            _pick(w12, l, k),
            _pick(w3, l, k),
            _pick(ln_g, l, 3 * k),
            _pick(ln_b, l, 3 * k),
        ],
        out_specs=[
            pl.BlockSpec((tm, d), lambda i: (i, 0)),
            pl.BlockSpec((tm, d), lambda i: (i, 0)),
        ],
        out_shape=[jax.ShapeDtypeStruct((t, d), F32), jax.ShapeDtypeStruct((t, d), BF16)],
        compiler_params=_params("parallel"),
        name="ffn_ln",
    )(x, w12, w3, ln_g, ln_b)


_HW = DN_HEADS * DN_DK
_C_QKV = 0
_C_Z = 3 * _HW
_C_BA = _C_Z + _HW
_C_LX = _C_BA + LANES
_C_LG = _C_LX + LRU_W
_C_ATT = _C_LG + LRU_W
_C_GL = _C_ATT + 3 * LA_HEADS * LA_HD
_SEG = 256
_N_TAILS = (3 * _HW + LRU_W) // _SEG


def _causal_conv(y, tail, cw):
    row8 = lax.broadcasted_iota(jnp.int32, (SUBLANES, y.shape[1]), 0)
    acc = y * cw[CONV_W - 1:CONV_W, :]
    fix = jnp.zeros((SUBLANES, y.shape[1]), F32)
    for k in range(1, CONV_W):
        wk = cw[CONV_W - 1 - k:CONV_W - k, :]
        sh = pltpu.roll(y, k, axis=0)
        acc = acc + sh * wk
        prev = pltpu.roll(tail, k, axis=0)
        fix = fix + jnp.where(row8 < k, (prev - sh[:SUBLANES]) * wk, 0.0)
    return jnp.concatenate([acc[:SUBLANES] + fix, acc[SUBLANES:]], axis=0)


def _mixer_in_kernel(tiles_per_seq, x_ref, w_ref, dcw_ref, lcw_ref, lcb_ref,
                     qkvz_ref, ba_ref, xg_ref, att_ref, gt_ref, tail_ref):
    xb = x_ref[...]
    first = (pl.program_id(0) % tiles_per_seq) == 0
    rows = xb.shape[0]

    def proj(c0, width=_SEG):
        return _dot(xb, w_ref[:, c0:c0 + width])

    def cols(j):
        return slice(j * _SEG, (j + 1) * _SEG)

    def conv(y, slot, cw):
        tail = jnp.where(first, 0.0, tail_ref[slot])
        tail_ref[slot] = y[rows - SUBLANES:]
        return _causal_conv(y, tail, cw)

    def delta_qkv(j):
        c = conv(proj(_C_QKV + j * _SEG), j, dcw_ref[:, cols(j)])
        c = c * _sigmoid(c)
        if j * _SEG < 2 * _HW:
            scale = np.float32(DN_DK ** -0.5 if j * _SEG < _HW else 1.0)
            parts = []
            for h in range(_SEG // DN_DK):
                ch = c[:, h * DN_DK:(h + 1) * DN_DK]
                parts.append(ch * (lax.rsqrt(jnp.sum(ch * ch, axis=-1, keepdims=True) + NORM_EPS) * scale))
            c = jnp.concatenate(parts, axis=1)
        qkvz_ref[:, cols(j)] = c.astype(BF16)

    def z_part(j):
        qkvz_ref[:, 3 * _HW + j * _SEG:3 * _HW + (j + 1) * _SEG] = proj(_C_Z + j * _SEG).astype(BF16)

    def lru_x(j):
        slot = 3 * _HW // _SEG + j
        xg_ref[:, cols(j)] = (conv(proj(_C_LX + j * _SEG), slot, lcw_ref[:, cols(j)])
                              + lcb_ref[:, cols(j)]).astype(BF16)

    def lru_g(j):
        xg_ref[:, LRU_W + j * _SEG:LRU_W + (j + 1) * _SEG] = _gelu_tanh(proj(_C_LG + j * _SEG)).astype(BF16)

    def gate(j):
        gt_ref[:, cols(j)] = _sigmoid(proj(_C_GL + j * _SEG)).astype(BF16)

    def att(j):
        att_ref[:, cols(j)] = proj(_C_ATT + j * _SEG).astype(BF16)

    heavy = ([functools.partial(delta_qkv, j) for j in range(3 * _HW // _SEG)]
             + [functools.partial(lru_x, j) for j in range(LRU_W // _SEG)]
             + [functools.partial(lru_g, j) for j in range(LRU_W // _SEG)])
    light = ([functools.partial(gate, j) for j in range(N_BRANCH * D_MODEL // _SEG)]
             + [functools.partial(att, j) for j in range(3 * _HW // _SEG)]
             + [functools.partial(z_part, j) for j in range(_HW // _SEG)])
    per_heavy = len(light) // len(heavy)
    for i, fn in enumerate(heavy):
        fn()
        for g in light[i * per_heavy:(i + 1) * per_heavy]:
            g()
    for g in light[len(heavy) * per_heavy:]:
        g()
    ba_ref[...] = proj(_C_BA, LANES)


def _mixer_in(xb, w, dcw, lcw, lcb, l, *, seq, tm=512):
    t, d = xb.shape

    def tile(width):
        return pl.BlockSpec((tm, width), lambda i: (i, 0))

    widths = (4 * _HW, LANES, 2 * LRU_W, 3 * _HW, N_BRANCH * D_MODEL)
    dtypes = (BF16, F32, BF16, BF16, BF16)
    return pl.pallas_call(
        functools.partial(_mixer_in_kernel, seq // tm),
        grid=(t // tm,),
        in_specs=[tile(d), _pick(w, l), _pick(dcw, l), _pick(lcw, l), _pick(lcb, l)],
        out_specs=[tile(wd) for wd in widths],
        out_shape=[jax.ShapeDtypeStruct((t, wd), dt) for wd, dt in zip(widths, dtypes)],
        scratch_shapes=[pltpu.VMEM((_N_TAILS, SUBLANES, _SEG), F32)],
        compiler_params=_params("arbitrary"),
        name="mixer_in",
    )(xb, w, dcw, lcw, lcb)


def _proj_plain_kernel(x_ref, w_ref, o_ref):
    o_ref[...] = _dot(x_ref[...], w_ref[...]).astype(o_ref.dtype)


def _proj_plain(xb, w, l, *, tn, out_dtype, name):
    t, d = xb.shape
    n = w.shape[2]
    return pl.pallas_call(
        _proj_plain_kernel,
        grid=(n // tn,),
        in_specs=[pl.BlockSpec((t, d), lambda j: (0, 0)), pl.BlockSpec((None, d, tn), lambda j: (l, 0, j))],
        out_specs=pl.BlockSpec((t, tn), lambda j: (0, j)),
        out_shape=jax.ShapeDtypeStruct((t, n), out_dtype),
        compiler_params=_params("parallel"),
        name=name,
    )(xb, w)


_PREP_CHUNKS = 4
_SCAN_CHUNKS = 2


def _segmented_cumsum_rows(x, seg):
    row = lax.broadcasted_iota(jnp.int32, x.shape, 0) & (seg - 1)
    d = 1
    while d < seg:
        x = x + jnp.where(row >= d, pltpu.roll(x, d, axis=0), 0.0)
        d *= 2
    return x


def _delta_prep_kernel(q_ref, k_ref, v_ref, ba_ref, alog_ref, dtb_ref,
                       wq_ref, u_ref, qk_ref, kdt_ref, gl_ref):
    row = lax.broadcasted_iota(jnp.int32, (CHUNK, CHUNK), 0)
    col = lax.broadcasted_iota(jnp.int32, (CHUNK, CHUNK), 1)
    lower = row >= col
    strict = row > col
    eye = jnp.where(row == col, 1.0, 0.0).astype(F32)
    zpad = jnp.zeros((LANES - CHUNK, LANES), F32)

    ba = ba_ref[0]
    beta_all = _sigmoid(ba)
    g_all = -jnp.exp(alog_ref[...]) * _softplus(ba + dtb_ref[...])
    g_cum = _segmented_cumsum_rows(g_all, CHUNK)
    g_cum_t = [jnp.transpose(g_cum[pr * LANES:(pr + 1) * LANES]) for pr in range(_PREP_CHUNKS // 2)]

    inst = [(ci, h) for ci in range(_PREP_CHUNKS) for h in range(DN_HEADS)]
    st = []
    for ci, h in inst:
        rs = slice(ci * CHUNK, (ci + 1) * CHUNK)
        hs = slice(h * DN_DK, (h + 1) * DN_DK)
        q = q_ref[0, rs, hs].astype(F32)
        k = k_ref[0, rs, hs].astype(F32)
        beta = beta_all[rs, h:h + 1]
        gc = g_cum[rs, DN_HEADS + h:DN_HEADS + h + 1]
        gr = g_cum_t[ci // 2][DN_HEADS + h:DN_HEADS + h + 1, (ci % 2) * CHUNK:(ci % 2 + 1) * CHUNK]
        decay = jnp.exp(jnp.where(lower, gc - gr, NEG_INF))
        kb = k * beta
        aq = _dot_nt(jnp.concatenate([kb, q], axis=0).astype(BF16), k.astype(BF16))
        st.append(dict(rs=rs, hs=hs, beta=beta, gc=gc, kb=kb, aq=aq, decay=decay))

    for s in st:
        s["p"] = -jnp.where(strict, s["aq"][:CHUNK] * s["decay"], 0.0)
        s["tinv"] = eye + s["p"]
    n = 1
    while 2 * n < CHUNK:
        for s in st:
            p16 = s["p"].astype(BF16)
            s["p"] = _dot(p16, p16)
        for s in st:
            s["tinv"] = s["tinv"] + _dot(s["tinv"].astype(BF16), s["p"].astype(BF16))
        n *= 2

    for s in st:
        exp_g = jnp.exp(s["gc"])
        v = v_ref[0, s["rs"], s["hs"]].astype(F32)
        rhs = jnp.concatenate([v * s["beta"], s["kb"] * exp_g], axis=1)
        s["sol"] = _dot(s["tinv"].astype(BF16), rhs.astype(BF16))
        s["exp_g"] = exp_g

    for (ci, h), s in zip(inst, st):
        q = q_ref[0, s["rs"], s["hs"]].astype(F32)
        k = k_ref[0, s["rs"], s["hs"]].astype(F32)
        g_last = s["gc"][CHUNK - 1:CHUNK, :]
        k_dec = k * jnp.exp(g_last - s["gc"])
        kdt = jnp.transpose(jnp.concatenate([k_dec, zpad], axis=0))[:, :CHUNK]
        wq_ref[0, ci, h * 2 * CHUNK:(h + 1) * 2 * CHUNK, :] = jnp.concatenate(
            [s["sol"][:, DN_DV:], q * s["exp_g"]], axis=0).astype(BF16)
        u_ref[0, ci, h * CHUNK:(h + 1) * CHUNK, :] = s["sol"][:, :DN_DV]
        qk_ref[0, ci, h * CHUNK:(h + 1) * CHUNK, :] = (s["aq"][CHUNK:] * s["decay"]).astype(BF16)
        kdt_ref[0, ci, h * DN_DK:(h + 1) * DN_DK, :] = kdt.astype(BF16)
        gl_ref[0, ci, h * SUBLANES:(h + 1) * SUBLANES, :] = jnp.broadcast_to(
            jnp.exp(g_last), (SUBLANES, LANES))


def _delta_prep(qkvz, ba, alog, dtb, l):
    batch, seq, _ = qkvz.shape
    n = seq // CHUNK
    rows = _PREP_CHUNKS * CHUNK
    hw = DN_HEADS * DN_DK

    def col_spec(cb):
        return pl.BlockSpec((1, rows, hw), lambda b, i: (b, i, cb))

    def out_spec(r, c):
        return pl.BlockSpec((1, _PREP_CHUNKS, r, c), lambda b, i: (b, i, 0, 0))

    return pl.pallas_call(
        _delta_prep_kernel,
        grid=(batch, n // _PREP_CHUNKS),
        in_specs=[col_spec(0), col_spec(1), col_spec(2),
                  pl.BlockSpec((1, rows, LANES), lambda b, i: (b, i, 0)),
                  _pick(alog, l), _pick(dtb, l)],
        out_specs=[out_spec(DN_HEADS * 2 * CHUNK, DN_DK), out_spec(DN_HEADS * CHUNK, DN_DV),
                   out_spec(DN_HEADS * CHUNK, CHUNK), out_spec(DN_HEADS * DN_DK, CHUNK),
                   out_spec(DN_HEADS * SUBLANES, LANES)],
        out_shape=[jax.ShapeDtypeStruct((batch, n, DN_HEADS * 2 * CHUNK, DN_DK), BF16),
                   jax.ShapeDtypeStruct((batch, n, DN_HEADS * CHUNK, DN_DV), F32),
                   jax.ShapeDtypeStruct((batch, n, DN_HEADS * CHUNK, CHUNK), BF16),
                   jax.ShapeDtypeStruct((batch, n, DN_HEADS * DN_DK, CHUNK), BF16),
                   jax.ShapeDtypeStruct((batch, n, DN_HEADS * SUBLANES, LANES), F32)],
        compiler_params=_params("parallel", "parallel"),
        name="delta_prep",
    )(qkvz, qkvz, qkvz, ba, alog, dtb)


def _delta_scan_kernel(batch, wq_ref, u_ref, qk_ref, kdt_ref, gl_ref, z_ref, nw_ref, o_ref, state_ref):
    c = pl.program_id(0)

    @pl.when(c == 0)
    def _():
        state_ref[...] = jnp.zeros_like(state_ref)

    chains = [(b, h) for b in range(batch) for h in range(DN_HEADS)]
    for cc in range(_SCAN_CHUNKS):
        rs = slice(cc * CHUNK, (cc + 1) * CHUNK)
        r = [_dot(wq_ref[b, cc, h * 2 * CHUNK:(h + 1) * 2 * CHUNK, :],
                  state_ref[b * DN_HEADS + h].astype(BF16)) for b, h in chains]
        v_new = [(u_ref[b, cc, h * CHUNK:(h + 1) * CHUNK, :] - r[i][:CHUNK]).astype(BF16)
                 for i, (b, h) in enumerate(chains)]
        kv = [_dot(kdt_ref[b, cc, h * DN_DK:(h + 1) * DN_DK, :], v_new[i]) for i, (b, h) in enumerate(chains)]
        qv = [_dot(qk_ref[b, cc, h * CHUNK:(h + 1) * CHUNK, :], v_new[i]) for i, (b, h) in enumerate(chains)]
        for i, (b, h) in enumerate(chains):
            state_ref[b * DN_HEADS + h] = (
                state_ref[b * DN_HEADS + h] * gl_ref[b, cc, h * SUBLANES:h * SUBLANES + 1, :] + kv[i])
            o = r[i][CHUNK:] + qv[i]
            z = z_ref[b, rs, h * DN_DV:(h + 1) * DN_DV].astype(F32)
            o = o * lax.rsqrt(jnp.mean(o * o, axis=-1, keepdims=True) + NORM_EPS)
            o = o * nw_ref[...] * (z * _sigmoid(z))
            o_ref[b, rs, h * DN_DV:(h + 1) * DN_DV] = o.astype(o_ref.dtype)


def _delta_scan(wq, u, qk, kdt, gl, qkvz, nw, l):
    batch, seq, _ = qkvz.shape
    hw = DN_HEADS * DN_DV

    rows = _SCAN_CHUNKS * CHUNK

    def step_spec(a):
        return pl.BlockSpec((batch, _SCAN_CHUNKS) + a.shape[2:], lambda c: (0, c, 0, 0))

    return pl.pallas_call(
        functools.partial(_delta_scan_kernel, batch),
        grid=(seq // rows,),
        in_specs=[step_spec(wq), step_spec(u), step_spec(qk), step_spec(kdt), step_spec(gl),
                  pl.BlockSpec((batch, rows, hw), lambda c: (0, c, 3)),
                  _pick(nw, l)],
        out_specs=pl.BlockSpec((batch, rows, hw), lambda c: (0, c, 0)),
        out_shape=jax.ShapeDtypeStruct((batch, seq, hw), BF16),
        scratch_shapes=[pltpu.VMEM((batch * DN_HEADS, DN_DK, DN_DV), F32)],
        compiler_params=_params("arbitrary"),
        name="delta_scan",
    )(wq, u, qk, kdt, gl, qkvz, nw)


def _lru_kernel(tiles_per_seq, xg_ref, wg_ref, gb_ref, lam_ref, o_ref, h_ref):
    i = pl.program_id(0)
    rows = xg_ref.shape[0]
    xc = xg_ref[:, :LRU_W].astype(F32)
    gate = xg_ref[:, LRU_W:].astype(F32)
    gates = _dot(xg_ref[:, :LRU_W], wg_ref[...]) + gb_ref[...]
    r = _sigmoid(gates[:, :LRU_W])
    ig = _sigmoid(gates[:, LRU_W:])
    log_a = -LRU_C * r * _softplus(-lam_ref[...])
    a = jnp.exp(log_a)
    u = xc * ig * jnp.sqrt(-jnp.tanh(log_a) * (1.0 + a * a))
    row = lax.broadcasted_iota(jnp.int32, a.shape, 0)
    d = 1
    while d < rows:
        a_sh = jnp.where(row >= d, pltpu.roll(a, d, axis=0), 1.0)
        u_sh = jnp.where(row >= d, pltpu.roll(u, d, axis=0), 0.0)
        u = a * u_sh + u
        a = a * a_sh
        d *= 2
    first = (i % tiles_per_seq) == 0
    h_prev = jnp.where(first, 0.0, h_ref[...])
    h = u + a * h_prev
    h_ref[...] = h[rows - 1:rows, :]
    o_ref[...] = (h * gate).astype(o_ref.dtype)


def _lru(xg, wg, gb, lam, l, *, seq, tm=256):
    t = xg.shape[0]
    return pl.pallas_call(
        functools.partial(_lru_kernel, seq // tm),
        grid=(t // tm,),
        in_specs=[
            pl.BlockSpec((tm, 2 * LRU_W), lambda i: (i, 0)),
            _pick(wg, l), _pick(gb, l), _pick(lam, l),
        ],
        out_specs=pl.BlockSpec((tm, LRU_W), lambda i: (i, 0)),
        out_shape=jax.ShapeDtypeStruct((t, LRU_W), BF16),
        scratch_shapes=[pltpu.VMEM((1, LRU_W), F32)],
        compiler_params=_params("arbitrary"),
        name="rg_lru",
    )(xg, wg, gb, lam)


_QB = 4 * CHUNK
_KB = 3
_CHUNK_SHIFT = CHUNK.bit_length() - 1


def _band_kernel(q_ref, k0_ref, k1_ref, k2_ref, v0_ref, v1_ref, v2_ref, rel_ref, o_ref, bias_ref):
    i = pl.program_id(1)

    @pl.when((pl.program_id(0) == 0) & (i == 0))
    def _():
        qpos = lax.broadcasted_iota(jnp.int32, (_QB, _QB), 0)
        kcol = lax.broadcasted_iota(jnp.int32, (_QB, _QB), 1)
        for m in range(_KB):
            chunk_off = ((kcol + (m - (_KB - 1)) * _QB) >> _CHUNK_SHIFT) - (qpos >> _CHUNK_SHIFT)
            valid = (chunk_off <= 0) & (chunk_off >= -LA_PAST)
            for h in range(LA_HEADS):
                ev = jnp.broadcast_to(rel_ref[h, m:m + 1, :], (_QB, 2 * _QB))
                toeplitz = pltpu.roll(ev, _QB, axis=1, stride=1, stride_axis=0)[:, :_QB]
                bias_ref[h, m] = jnp.where(valid, toeplitz, NEG_INF)

    k_refs = (k0_ref, k1_ref, k2_ref)
    v_refs = (v0_ref, v1_ref, v2_ref)
    lane = lax.broadcasted_iota(jnp.int32, (1, LANES), 1)
    half_sel = (lane < LA_HD, lane >= LA_HD)

    def attend(ms):
        def scores(h):
            ps = slice((h // 2) * LANES, (h // 2 + 1) * LANES)
            qh = jnp.where(half_sel[h % 2], q_ref[0, :, ps], jnp.zeros((), BF16))
            return [_dot_nt(qh, k_refs[m][0, :, ps]) for m in ms]

        s_next = scores(0)
        pair_out = None
        for h in range(LA_HEADS):
            ps = slice((h // 2) * LANES, (h // 2 + 1) * LANES)
            sel = half_sel[h % 2]
            s = s_next
            if h + 1 < LA_HEADS:
                s_next = scores(h + 1)
            s = [sm + bias_ref[h, m] for sm, m in zip(s, ms)]
            mx = s[0].max(axis=-1, keepdims=True)
            for sm in s[1:]:
                mx = jnp.maximum(mx, sm.max(axis=-1, keepdims=True))
            acc = None
            for sm, m in zip(s, ms):
                vh = jnp.where(sel, v_refs[m][0, :, ps], jnp.ones((), BF16))
                part = _dot(jnp.exp((sm - mx).astype(BF16)), vh)
                acc = part if acc is None else acc + part
            den = pltpu.roll(acc, LA_HD, axis=1)
            out = jnp.where(sel, acc / den, 0.0)
            if h % 2 == 0:
                pair_out = out
            else:
                o_ref[0, :, ps] = (pair_out + out).astype(o_ref.dtype)

    for nvalid in range(1, _KB + 1):
        pl.when(jnp.minimum(i, _KB - 1) == nvalid - 1)(
            functools.partial(attend, tuple(range(_KB - nvalid, _KB))))


def _band_attention(qkv, rel_rows, l, *, batch, seq):
    hw = LA_HEADS * LA_HD
    nblk = seq // _QB

    def kv_spec(col, m):
        return pl.BlockSpec((1, _QB, hw), lambda b, i: (b, jnp.maximum(i - (_KB - 1 - m), 0), col))

    return pl.pallas_call(
        _band_kernel,
        grid=(batch, nblk),
        in_specs=[pl.BlockSpec((1, _QB, hw), lambda b, i: (b, i, 0))]
        + [kv_spec(1, m) for m in range(_KB)]
        + [kv_spec(2, m) for m in range(_KB)]
        + [_pick(rel_rows, l)],
        out_specs=pl.BlockSpec((1, _QB, hw), lambda b, i: (b, i, 0)),
        out_shape=jax.ShapeDtypeStruct((batch, seq, hw), BF16),
        scratch_shapes=[pltpu.VMEM((LA_HEADS, _KB, _QB, _QB), F32)],
        compiler_params=_params("arbitrary", "arbitrary"),
        name="band_attention",
    )(qkv, qkv, qkv, qkv, qkv, qkv, qkv, rel_rows)


def _band_rel_rows(rel_table):
    t = rel_table.astype(F32)
    lo = _KB * _QB - REL_CLIP
    hi = _QB - 1 - REL_CLIP
    full = jnp.concatenate([jnp.broadcast_to(t[..., :1], t.shape[:-1] + (lo,)), t,
                            jnp.broadcast_to(t[..., -1:], t.shape[:-1] + (hi,))], axis=-1)
    return jnp.stack([full[..., m * _QB:(m + 2) * _QB] for m in range(_KB)], axis=-2)


_SUB_TILES = 2


def _sub_tiles(rows):
    step = rows // _SUB_TILES
    return [slice(s * step, (s + 1) * step) for s in range(_SUB_TILES)]


def _mix_out_kernel(ya_ref, yb_ref, yc_ref, gt_ref, x_ref, wb_ref, wo_ref, g_ref, b_ref, o_ref, ob_ref):
    ys = (ya_ref, yb_ref, yc_ref)
    subs = _sub_tiles(x_ref.shape[0])
    ups = [[_dot(ys[r][rs, :], wb_ref[r]) for r in range(N_BRANCH)] for rs in subs]
    merged = []
    for rs, up in zip(subs, ups):
        m = None
        for r in range(N_BRANCH):
            term = gt_ref[rs, r * D_MODEL:(r + 1) * D_MODEL].astype(F32) * up[r]
            m = term if m is None else m + term
        merged.append(m.astype(BF16))
    y = [_dot(m, wo_ref[...]) for m in merged]
    for rs, yy in zip(subs, y):
        out = _layer_norm(ALPHA * x_ref[rs, :] + yy, g_ref[...], b_ref[...])
        o_ref[rs, :] = out
        ob_ref[rs, :] = out.astype(BF16)


def _mix_out(ya, yb, yc, gates, x, wb, wo, ln_g, ln_b, l, *, tm=512):
    t, d = x.shape
    return pl.pallas_call(
        _mix_out_kernel,
        grid=(t // tm,),
        in_specs=[
            pl.BlockSpec((tm, BR_W), lambda i: (i, 0)),
            pl.BlockSpec((tm, BR_W), lambda i: (i, 0)),
            pl.BlockSpec((tm, BR_W), lambda i: (i, 0)),
            pl.BlockSpec((tm, N_BRANCH * d), lambda i: (i, 0)),
            pl.BlockSpec((tm, d), lambda i: (i, 0)),
            _pick(wb, l), _pick(wo, l), _pick(ln_g, l, 1), _pick(ln_b, l, 1),
        ],
        out_specs=[pl.BlockSpec((tm, d), lambda i: (i, 0)), pl.BlockSpec((tm, d), lambda i: (i, 0))],
        out_shape=[jax.ShapeDtypeStruct((t, d), F32), jax.ShapeDtypeStruct((t, d), BF16)],
        compiler_params=_params("parallel"),
        name="mix_out",
    )(ya, yb, yc, gates, x, wb, wo, ln_g, ln_b)


def _xattn_kernel(xb_ref, x_ref, kv_ref, wq_ref, wo_ref, g_ref, b_ref, o_ref, ob_ref):
    subs = _sub_tiles(x_ref.shape[1])
    q = [(_dot(xb_ref[0, rs, :], wq_ref[...]) * np.float32(XA_HD ** -0.5)).astype(BF16) for rs in subs]

    def scores(task):
        s, h = task
        sl = slice(h * XA_HD, (h + 1) * XA_HD)
        return _dot_nt(q[s][:, sl], kv_ref[0, :, sl])

    tasks = [(s, h) for h in range(XA_HEADS) for s in range(len(subs))]
    outs = [[None] * XA_HEADS for _ in subs]
    s_next = scores(tasks[0])
    for ti, (s, h) in enumerate(tasks):
        v = kv_ref[0, :, D_MODEL + h * XA_HD:D_MODEL + (h + 1) * XA_HD]
        sc = s_next
        if ti + 1 < len(tasks):
            s_next = scores(tasks[ti + 1])
        p = jnp.exp(sc - sc.max(axis=-1, keepdims=True))
        den = p.sum(axis=-1, keepdims=True)
        outs[s][h] = (_dot(p.astype(BF16), v) / den).astype(BF16)
    y = [_dot(jnp.concatenate(o, axis=1), wo_ref[...]) for o in outs]
    for rs, yy in zip(subs, y):
        out = _layer_norm(ALPHA * x_ref[0, rs, :] + yy, g_ref[...], b_ref[...])
        o_ref[0, rs, :] = out
        ob_ref[0, rs, :] = out.astype(BF16)


def _xattn(xb, x, kv, wq, wo, ln_g, ln_b, l, *, tm=512):
    batch, seq, d = x.shape
    return pl.pallas_call(
        _xattn_kernel,
        grid=(batch, seq // tm),
        in_specs=[
            pl.BlockSpec((1, tm, d), lambda bi, i: (bi, i, 0)),
            pl.BlockSpec((1, tm, d), lambda bi, i: (bi, i, 0)),
            pl.BlockSpec((1, MEM_LEN, 2 * d), lambda bi, i: (bi, 0, 0)),
            _pick(wq, l), _pick(wo, l), _pick(ln_g, l, 2), _pick(ln_b, l, 2),
        ],
        out_specs=[
            pl.BlockSpec((1, tm, d), lambda bi, i: (bi, i, 0)),
            pl.BlockSpec((1, tm, d), lambda bi, i: (bi, i, 0)),
        ],
        out_shape=[jax.ShapeDtypeStruct((batch, seq, d), F32), jax.ShapeDtypeStruct((batch, seq, d), BF16)],
        compiler_params=_params("parallel", "parallel"),
        name="mem_xattn",
    )(xb, x, kv, wq, wo, ln_g, ln_b)


def _prepare(p):
    d = D_MODEL
    depth = p["ln_g"].shape[0]
    w_in = p["mix_w_in"]
    att_scale = jnp.concatenate([jnp.full((LA_HEADS * LA_HD,), LA_HD ** -0.5, F32),
                                 jnp.ones((2 * LA_HEADS * LA_HD,), F32)])
    w_packed = jnp.concatenate([
        w_in[:, :, :_O_DB],
        jnp.pad(w_in[:, :, _O_DB:_O_LX], ((0, 0), (0, 0), (0, LANES - 2 * DN_HEADS))),
        w_in[:, :, _O_LX:_O_AQ],
        w_in[:, :, _O_AQ:_O_GL] * att_scale,
        w_in[:, :, _O_GL:]], axis=2).astype(BF16)

    def lane_row(v):
        return jnp.pad(v.astype(F32), ((0, 0), (DN_HEADS, LANES - 2 * DN_HEADS)))[:, None, :]

    eye = jnp.eye(LRU_BLOCKS, dtype=F32)
    wg = jnp.einsum("lgnde,nm->lgndme", p["lru_gate_w"].astype(F32), eye)
    wg = wg.reshape(depth, 2, LRU_W, LRU_W)
    wg = jnp.concatenate([wg[:, 0], wg[:, 1]], axis=2).astype(BF16)
    return dict(
        ln_g=p["ln_g"].astype(F32).reshape(depth, 4, 1, d),
        ln_b=p["ln_b"].astype(F32).reshape(depth, 4, 1, d),
        w12=p["ffn_w12"].astype(BF16), w3=p["ffn_w3"].astype(BF16),
        w_in=w_packed,
        dcw=p["dn_conv_w"].astype(F32), lcw=p["lru_conv_w"].astype(F32),
        lcb=p["lru_conv_b"].astype(F32)[:, None, :],
        alog=lane_row(p["dn_a_log"]), dtb=lane_row(p["dn_dt_bias"]),
        nw=p["dn_norm_w"].astype(F32)[:, None, :],
        wg=wg, gb=p["lru_gate_b"].astype(F32).reshape(depth, 1, 2 * LRU_W),
        lam=p["lru_lambda"].astype(F32)[:, None, :],
        rel=_band_rel_rows(p["la_rel_bias"]),
        wb=p["w_branch"].astype(BF16), wout=p["mix_w_out"].astype(BF16),
        wq=p["xa_wq"].astype(BF16), wkv=p["xa_wkv"].astype(BF16), wo=p["xa_wo"].astype(BF16))


def _layer(x, mem_b, l, w):
    batch, seq, d = x.shape
    t = batch * seq

    x, xb = _ffn(x.reshape(t, d), w["w12"], w["w3"], w["ln_g"], w["ln_b"], (l, 0))

    qkvz, ba, xg, aqkv, gates = _mixer_in(xb, w["w_in"], w["dcw"], w["lcw"], w["lcb"], l, seq=seq)
    qkvz = qkvz.reshape(batch, seq, -1)
    factors = _delta_prep(qkvz, ba.reshape(batch, seq, LANES), w["alog"], w["dtb"], l)
    ya = _delta_scan(*factors, qkvz, w["nw"], l)
    yb = _lru(xg, w["wg"], w["gb"], w["lam"], l, seq=seq)
    yc = _band_attention(aqkv.reshape(batch, seq, -1), w["rel"], l, batch=batch, seq=seq)
    x, xb = _mix_out(ya.reshape(t, -1), yb, yc.reshape(t, -1), gates, x, w["wb"], w["wout"],
                     w["ln_g"], w["ln_b"], l)

    kv = _proj_plain(mem_b, w["wkv"], l, tn=D_MODEL, out_dtype=BF16, name="proj_kv")
    x, xb = _xattn(xb.reshape(batch, seq, d), x.reshape(batch, seq, d), kv.reshape(batch, MEM_LEN, 2 * d),
                   w["wq"], w["wo"], w["ln_g"], w["ln_b"], l)

    x, _ = _ffn(x.reshape(t, d), w["w12"], w["w3"], w["ln_g"], w["ln_b"], (l, 1))
    return x.reshape(batch, seq, d)


def kernel(x, mem, ln_g, ln_b, ffn_w12, ffn_w3, mix_w_in, dn_conv_w, dn_a_log, dn_dt_bias, dn_norm_w,
           lru_conv_w, lru_conv_b, lru_gate_w, lru_gate_b, lru_lambda, la_rel_bias, w_branch, mix_w_out,
           xa_wq, xa_wkv, xa_wo):
    w = _prepare(dict(
        ln_g=ln_g, ln_b=ln_b, ffn_w12=ffn_w12, ffn_w3=ffn_w3, mix_w_in=mix_w_in, dn_conv_w=dn_conv_w,
        dn_a_log=dn_a_log, dn_dt_bias=dn_dt_bias, dn_norm_w=dn_norm_w, lru_conv_w=lru_conv_w,
        lru_conv_b=lru_conv_b, lru_gate_w=lru_gate_w, lru_gate_b=lru_gate_b, lru_lambda=lru_lambda,
        la_rel_bias=la_rel_bias, w_branch=w_branch, mix_w_out=mix_w_out, xa_wq=xa_wq, xa_wkv=xa_wkv,
        xa_wo=xa_wo))
    batch = x.shape[0]
    mem_b = mem.reshape(batch * MEM_LEN, D_MODEL).astype(BF16)
    x = x.astype(F32)
    for l in range(DEPTH):
        x = _layer(x, mem_b, l, w)
    return x
```

```python
import functools

import numpy as np
import jax
import jax.numpy as jnp
from jax import lax
from jax.experimental import pallas as pl
from jax.experimental.pallas import tpu as pltpu

F32 = jnp.float32
BF16 = jnp.bfloat16

D_MODEL = 1024
DEPTH = 2
CHUNK = 64
CONV_W = 4
BR_W = 512
N_BRANCH = 3
DN_HEADS = 4
DN_DK = 128
DN_DV = 128
LRU_W = BR_W
LRU_BLOCKS = 8
LRU_BLK = LRU_W // LRU_BLOCKS
LRU_C = 8.0
LA_HEADS = 8
LA_HD = 64
LA_PAST = 8
REL_CLIP = 128
MEM_LEN = 256
XA_HEADS = 4
XA_HD = D_MODEL // XA_HEADS
D_FF = 2816
ALPHA = (2 * DEPTH) ** 0.25
LN_EPS = 1e-5
NORM_EPS = 1e-6
NEG_INF = -1e30

LANES = 128
SUBLANES = 8
VMEM_LIMIT = 56 * 1024 * 1024

_O_DZ = 3 * DN_HEADS * DN_DK
_O_DB = _O_DZ + DN_HEADS * DN_DV
_O_LX = _O_DB + 2 * DN_HEADS
_O_AQ = _O_LX + 2 * LRU_W
_O_GL = _O_AQ + 3 * LA_HEADS * LA_HD


def _params(*sem):
    return pltpu.CompilerParams(dimension_semantics=sem, vmem_limit_bytes=VMEM_LIMIT)


def _dot(a, b):
    return jnp.dot(a, b, preferred_element_type=F32)


def _dot_nt(a, b):
    return lax.dot_general(a, b, (((1,), (1,)), ((), ())), preferred_element_type=F32)


def _sigmoid(x):
    return 1.0 / (1.0 + jnp.exp(-x))


def _softplus(x):
    return jnp.maximum(x, 0.0) + jnp.log1p(jnp.exp(-jnp.abs(x)))


def _gelu_tanh(x):
    c = np.float32(np.sqrt(2.0 / np.pi))
    return 0.5 * x * (1.0 + jnp.tanh(c * (x + np.float32(0.044715) * (x * x * x))))


def _layer_norm(y, g, b):
    mu = jnp.mean(y, axis=-1, keepdims=True)
    d = y - mu
    var = jnp.mean(d * d, axis=-1, keepdims=True)
    return d * lax.rsqrt(var + LN_EPS) * g + b


def _pick(arr, *idx):
    shape = (None,) * len(idx) + tuple(arr.shape[len(idx):])
    index = tuple(idx) + (0,) * (arr.ndim - len(idx))
    return pl.BlockSpec(shape, lambda *_: index, pipeline_mode=pl.Buffered(1))


_FF_CHUNK = 256


def _ffn_kernel(x_ref, w12_ref, w3_ref, g_ref, b_ref, o_ref, ob_ref):
    x = x_ref[...]
    xb = x.astype(BF16)
    n = D_FF // _FF_CHUNK

    def gate_up(c):
        lo = c * _FF_CHUNK
        return (_dot(xb, w12_ref[:, lo:lo + _FF_CHUNK]),
                _dot(xb, w12_ref[:, D_FF + lo:D_FF + lo + _FF_CHUNK]))

    nxt = gate_up(0)
    acc = None
    for c in range(n):
        g, u = nxt
        if c + 1 < n:
            nxt = gate_up(c + 1)
        h = (g * _sigmoid(g) * u).astype(BF16)
        part = _dot(h, w3_ref[c * _FF_CHUNK:(c + 1) * _FF_CHUNK, :])
        acc = part if acc is None else acc + part
    out = _layer_norm(ALPHA * x + 0.5 * acc, g_ref[...], b_ref[...])
    o_ref[...] = out
    ob_ref[...] = out.astype(BF16)


def _ffn(x, w12, w3, ln_g, ln_b, sel, *, tm=512):
    t, d = x.shape
    l, k = sel
    return pl.pallas_call(
        _ffn_kernel,
        grid=(t // tm,),
        in_specs=[
            pl.BlockSpec((tm, d), lambda i: (i, 0)),
            _pick(w12, l, k),
            _pick(w3, l, k),
            _pick(ln_g, l, 3 * k),
            _pick(ln_b, l, 3 * k),
        ],
        out_specs=[
            pl.BlockSpec((tm, d), lambda i: (i, 0)),
            pl.BlockSpec((tm, d), lambda i: (i, 0)),
        ],
        out_shape=[jax.ShapeDtypeStruct((t, d), F32), jax.ShapeDtypeStruct((t, d), BF16)],
        compiler_params=_params("parallel"),
        name="ffn_ln",
    )(x, w12, w3, ln_g, ln_b)


_HW = DN_HEADS * DN_DK
_C_QKV = 0
_C_Z = 3 * _HW
_C_BA = _C_Z + _HW
_C_LX = _C_BA + LANES
_C_LG = _C_LX + LRU_W
_C_ATT = _C_LG + LRU_W
_SEG = 256
_N_TAILS = (3 * _HW + LRU_W) // _SEG


def _causal_conv(y, tail, cw):
    row8 = lax.broadcasted_iota(jnp.int32, (SUBLANES, y.shape[1]), 0)
    acc = y * cw[CONV_W - 1:CONV_W, :]
    fix = jnp.zeros((SUBLANES, y.shape[1]), F32)
    for k in range(1, CONV_W):
        wk = cw[CONV_W - 1 - k:CONV_W - k, :]
        sh = pltpu.roll(y, k, axis=0)
        acc = acc + sh * wk
        prev = pltpu.roll(tail, k, axis=0)
        fix = fix + jnp.where(row8 < k, (prev - sh[:SUBLANES]) * wk, 0.0)
    return jnp.concatenate([acc[:SUBLANES] + fix, acc[SUBLANES:]], axis=0)


def _mixer_in_kernel(tiles_per_seq, x_ref, w_ref, dcw_ref, lcw_ref, lcb_ref,
                     qkvz_ref, ba_ref, xg_ref, att_ref, tail_ref):
    xb = x_ref[...]
    first = (pl.program_id(0) % tiles_per_seq) == 0
    rows = xb.shape[0]

    def proj(c0, width=_SEG):
        return _dot(xb, w_ref[:, c0:c0 + width])

    def cols(j):
        return slice(j * _SEG, (j + 1) * _SEG)

    def conv(y, slot, cw):
        tail = jnp.where(first, 0.0, tail_ref[slot])
        tail_ref[slot] = y[rows - SUBLANES:]
        return _causal_conv(y, tail, cw)

    def delta_qkv(j):
        c = conv(proj(_C_QKV + j * _SEG), j, dcw_ref[:, cols(j)])
        c = c * _sigmoid(c)
        if j * _SEG < 2 * _HW:
            scale = np.float32(DN_DK ** -0.5 if j * _SEG < _HW else 1.0)
            parts = []
            for h in range(_SEG // DN_DK):
                ch = c[:, h * DN_DK:(h + 1) * DN_DK]
                parts.append(ch * (lax.rsqrt(jnp.sum(ch * ch, axis=-1, keepdims=True) + NORM_EPS) * scale))
            c = jnp.concatenate(parts, axis=1)
        qkvz_ref[:, cols(j)] = c.astype(BF16)

    def z_part(j):
        qkvz_ref[:, 3 * _HW + j * _SEG:3 * _HW + (j + 1) * _SEG] = proj(_C_Z + j * _SEG).astype(BF16)

    def lru_x(j):
        slot = 3 * _HW // _SEG + j
        xg_ref[:, cols(j)] = (conv(proj(_C_LX + j * _SEG), slot, lcw_ref[:, cols(j)])
                              + lcb_ref[:, cols(j)]).astype(BF16)

    def lru_g(j):
        xg_ref[:, LRU_W + j * _SEG:LRU_W + (j + 1) * _SEG] = _gelu_tanh(proj(_C_LG + j * _SEG)).astype(BF16)

    def att(j):
        att_ref[:, cols(j)] = proj(_C_ATT + j * _SEG).astype(BF16)

    heavy = ([functools.partial(delta_qkv, j) for j in range(3 * _HW // _SEG)]
             + [functools.partial(lru_x, j) for j in range(LRU_W // _SEG)]
             + [functools.partial(lru_g, j) for j in range(LRU_W // _SEG)])
    light = ([functools.partial(att, j) for j in range(3 * _HW // _SEG)]
             + [functools.partial(z_part, j) for j in range(_HW // _SEG)])
    for i, fn in enumerate(heavy):
        fn()
        if i < len(light):
            light[i]()
    for g in light[len(heavy):]:
        g()
    ba_ref[...] = proj(_C_BA, LANES)


def _mixer_in(xb, w, dcw, lcw, lcb, l, *, seq, tm=512):
    t, d = xb.shape

    def tile(width):
        return pl.BlockSpec((tm, width), lambda i: (i, 0))

    widths = (4 * _HW, LANES, 2 * LRU_W, 3 * _HW)
    dtypes = (BF16, F32, BF16, BF16)
    return pl.pallas_call(
        functools.partial(_mixer_in_kernel, seq // tm),
        grid=(t // tm,),
        in_specs=[tile(d), _pick(w, l), _pick(dcw, l), _pick(lcw, l), _pick(lcb, l)],
        out_specs=[tile(wd) for wd in widths],
        out_shape=[jax.ShapeDtypeStruct((t, wd), dt) for wd, dt in zip(widths, dtypes)],
        scratch_shapes=[pltpu.VMEM((_N_TAILS, SUBLANES, _SEG), F32)],
        compiler_params=_params("arbitrary"),
        name="mixer_in",
    )(xb, w, dcw, lcw, lcb)


def _proj_plain_kernel(x_ref, w_ref, o_ref):
    o_ref[...] = _dot(x_ref[...], w_ref[...]).astype(o_ref.dtype)


def _proj_plain(xb, w, l, *, tn, out_dtype, name):
    t, d = xb.shape
    n = w.shape[2]
    return pl.pallas_call(
        _proj_plain_kernel,
        grid=(n // tn,),
        in_specs=[pl.BlockSpec((t, d), lambda j: (0, 0)), pl.BlockSpec((None, d, tn), lambda j: (l, 0, j))],
        out_specs=pl.BlockSpec((t, tn), lambda j: (0, j)),
        out_shape=jax.ShapeDtypeStruct((t, n), out_dtype),
        compiler_params=_params("parallel"),
        name=name,
    )(xb, w)


_PREP_CHUNKS = 4
_SCAN_CHUNKS = 4


def _segmented_cumsum_rows(x, seg):
    row = lax.broadcasted_iota(jnp.int32, x.shape, 0) & (seg - 1)
    d = 1
    while d < seg:
        x = x + jnp.where(row >= d, pltpu.roll(x, d, axis=0), 0.0)
        d *= 2
    return x


def _delta_prep_kernel(q_ref, k_ref, v_ref, ba_ref, alog_ref, dtb_ref,
                       wq_ref, u_ref, qk_ref, kdt_ref, gl_ref):
    row = lax.broadcasted_iota(jnp.int32, (CHUNK, CHUNK), 0)
    col = lax.broadcasted_iota(jnp.int32, (CHUNK, CHUNK), 1)
    lower = row >= col
    strict = row > col
    eye = jnp.where(row == col, 1.0, 0.0).astype(F32)
    zpad = jnp.zeros((LANES - CHUNK, LANES), F32)

    ba = ba_ref[0]
    beta_all = _sigmoid(ba)
    g_all = -jnp.exp(alog_ref[...]) * _softplus(ba + dtb_ref[...])
    g_cum = _segmented_cumsum_rows(g_all, CHUNK)
    g_cum_t = [jnp.transpose(g_cum[pr * LANES:(pr + 1) * LANES]) for pr in range(_PREP_CHUNKS // 2)]

    inst = [(ci, h) for ci in range(_PREP_CHUNKS) for h in range(DN_HEADS)]
    st = []
    for ci, h in inst:
        rs = slice(ci * CHUNK, (ci + 1) * CHUNK)
        hs = slice(h * DN_DK, (h + 1) * DN_DK)
        q = q_ref[0, rs, hs].astype(F32)
        k = k_ref[0, rs, hs].astype(F32)
        beta = beta_all[rs, h:h + 1]
        gc = g_cum[rs, DN_HEADS + h:DN_HEADS + h + 1]
        gr = g_cum_t[ci // 2][DN_HEADS + h:DN_HEADS + h + 1, (ci % 2) * CHUNK:(ci % 2 + 1) * CHUNK]
        decay = jnp.exp(jnp.where(lower, gc - gr, NEG_INF))
        kb = k * beta
        aq = _dot_nt(jnp.concatenate([kb, q], axis=0).astype(BF16), k.astype(BF16))
        st.append(dict(rs=rs, hs=hs, beta=beta, gc=gc, kb=kb, aq=aq, decay=decay))

    for s in st:
        s["p"] = -jnp.where(strict, s["aq"][:CHUNK] * s["decay"], 0.0)
        s["tinv"] = eye + s["p"]
    n = 1
    while 2 * n < CHUNK:
        for s in st:
            p16 = s["p"].astype(BF16)
            s["p"] = _dot(p16, p16)
        for s in st:
            s["tinv"] = s["tinv"] + _dot(s["tinv"].astype(BF16), s["p"].astype(BF16))
        n *= 2

    for s in st:
        exp_g = jnp.exp(s["gc"])
        v = v_ref[0, s["rs"], s["hs"]].astype(F32)
        rhs = jnp.concatenate([v * s["beta"], s["kb"] * exp_g], axis=1)
        s["sol"] = _dot(s["tinv"].astype(BF16), rhs.astype(BF16))
        s["exp_g"] = exp_g

    for (ci, h), s in zip(inst, st):
        q = q_ref[0, s["rs"], s["hs"]].astype(F32)
        k = k_ref[0, s["rs"], s["hs"]].astype(F32)
        g_last = s["gc"][CHUNK - 1:CHUNK, :]
        k_dec = k * jnp.exp(g_last - s["gc"])
        kdt = jnp.transpose(jnp.concatenate([k_dec, zpad], axis=0))[:, :CHUNK]
        wq_ref[0, ci, h * 2 * CHUNK:(h + 1) * 2 * CHUNK, :] = jnp.concatenate(
            [s["sol"][:, DN_DV:], q * s["exp_g"]], axis=0).astype(BF16)
        u_ref[0, ci, h * CHUNK:(h + 1) * CHUNK, :] = s["sol"][:, :DN_DV]
        qk_ref[0, ci, h * CHUNK:(h + 1) * CHUNK, :] = (s["aq"][CHUNK:] * s["decay"]).astype(BF16)
        kdt_ref[0, ci, h * DN_DK:(h + 1) * DN_DK, :] = kdt.astype(BF16)
        gl_ref[0, ci, h * SUBLANES:(h + 1) * SUBLANES, :] = jnp.broadcast_to(
            jnp.exp(g_last), (SUBLANES, LANES))


def _delta_prep(qkvz, ba, alog, dtb, l):
    batch, seq, _ = qkvz.shape
    n = seq // CHUNK
    rows = _PREP_CHUNKS * CHUNK
    hw = DN_HEADS * DN_DK

    def col_spec(cb):
        return pl.BlockSpec((1, rows, hw), lambda b, i: (b, i, cb))

    def out_spec(r, c):
        return pl.BlockSpec((1, _PREP_CHUNKS, r, c), lambda b, i: (b, i, 0, 0))

    return pl.pallas_call(
        _delta_prep_kernel,
        grid=(batch, n // _PREP_CHUNKS),
        in_specs=[col_spec(0), col_spec(1), col_spec(2),
                  pl.BlockSpec((1, rows, LANES), lambda b, i: (b, i, 0)),
                  _pick(alog, l), _pick(dtb, l)],
        out_specs=[out_spec(DN_HEADS * 2 * CHUNK, DN_DK), out_spec(DN_HEADS * CHUNK, DN_DV),
                   out_spec(DN_HEADS * CHUNK, CHUNK), out_spec(DN_HEADS * DN_DK, CHUNK),
                   out_spec(DN_HEADS * SUBLANES, LANES)],
        out_shape=[jax.ShapeDtypeStruct((batch, n, DN_HEADS * 2 * CHUNK, DN_DK), BF16),
                   jax.ShapeDtypeStruct((batch, n, DN_HEADS * CHUNK, DN_DV), F32),
                   jax.ShapeDtypeStruct((batch, n, DN_HEADS * CHUNK, CHUNK), BF16),
                   jax.ShapeDtypeStruct((batch, n, DN_HEADS * DN_DK, CHUNK), BF16),
                   jax.ShapeDtypeStruct((batch, n, DN_HEADS * SUBLANES, LANES), F32)],
        compiler_params=_params("parallel", "parallel"),
        name="delta_prep",
    )(qkvz, qkvz, qkvz, ba, alog, dtb)


def _delta_scan_kernel(batch, wq_ref, u_ref, qk_ref, kdt_ref, gl_ref, z_ref, nw_ref, o_ref, state_ref):
    c = pl.program_id(0)

    @pl.when(c == 0)
    def _():
        state_ref[...] = jnp.zeros_like(state_ref)

    chains = [(b, h) for b in range(batch) for h in range(DN_HEADS)]
    for cc in range(_SCAN_CHUNKS):
        rs = slice(cc * CHUNK, (cc + 1) * CHUNK)
        r = [_dot(wq_ref[b, cc, h * 2 * CHUNK:(h + 1) * 2 * CHUNK, :],
                  state_ref[b * DN_HEADS + h].astype(BF16)) for b, h in chains]
        v_new = [(u_ref[b, cc, h * CHUNK:(h + 1) * CHUNK, :] - r[i][:CHUNK]).astype(BF16)
                 for i, (b, h) in enumerate(chains)]
        kv = [_dot(kdt_ref[b, cc, h * DN_DK:(h + 1) * DN_DK, :], v_new[i]) for i, (b, h) in enumerate(chains)]
        qv = [_dot(qk_ref[b, cc, h * CHUNK:(h + 1) * CHUNK, :], v_new[i]) for i, (b, h) in enumerate(chains)]
        for i, (b, h) in enumerate(chains):
            state_ref[b * DN_HEADS + h] = (
                state_ref[b * DN_HEADS + h] * gl_ref[b, cc, h * SUBLANES:h * SUBLANES + 1, :] + kv[i])
            o = r[i][CHUNK:] + qv[i]
            z = z_ref[b, rs, h * DN_DV:(h + 1) * DN_DV].astype(F32)
            o = o * lax.rsqrt(jnp.mean(o * o, axis=-1, keepdims=True) + NORM_EPS)
            o = o * nw_ref[...] * (z * _sigmoid(z))
            o_ref[b, rs, h * DN_DV:(h + 1) * DN_DV] = o.astype(o_ref.dtype)


def _delta_scan(wq, u, qk, kdt, gl, qkvz, nw, l):
    batch, seq, _ = qkvz.shape
    hw = DN_HEADS * DN_DV

    rows = _SCAN_CHUNKS * CHUNK

    def step_spec(a):
        return pl.BlockSpec((batch, _SCAN_CHUNKS) + a.shape[2:], lambda c: (0, c, 0, 0))

    return pl.pallas_call(
        functools.partial(_delta_scan_kernel, batch),
        grid=(seq // rows,),
        in_specs=[step_spec(wq), step_spec(u), step_spec(qk), step_spec(kdt), step_spec(gl),
                  pl.BlockSpec((batch, rows, hw), lambda c: (0, c, 3)),
                  _pick(nw, l)],
        out_specs=pl.BlockSpec((batch, rows, hw), lambda c: (0, c, 0)),
        out_shape=jax.ShapeDtypeStruct((batch, seq, hw), BF16),
        scratch_shapes=[pltpu.VMEM((batch * DN_HEADS, DN_DK, DN_DV), F32)],
        compiler_params=_params("arbitrary"),
        name="delta_scan",
    )(wq, u, qk, kdt, gl, qkvz, nw)


def _lru_kernel(tiles_per_seq, xg_ref, wg_ref, gb_ref, lam_ref, o_ref, h_ref):
    i = pl.program_id(0)
    rows = xg_ref.shape[0]
    xc = xg_ref[:, :LRU_W].astype(F32)
    gate = xg_ref[:, LRU_W:].astype(F32)
    gates = _dot(xg_ref[:, :LRU_W], wg_ref[...]) + gb_ref[...]
    r = _sigmoid(gates[:, :LRU_W])
    ig = _sigmoid(gates[:, LRU_W:])
    log_a = -LRU_C * r * _softplus(-lam_ref[...])
    a = jnp.exp(log_a)
    u = xc * ig * jnp.sqrt(-jnp.tanh(log_a) * (1.0 + a * a))
    row = lax.broadcasted_iota(jnp.int32, a.shape, 0)
    d = 1
    while d < rows:
        a_sh = jnp.where(row >= d, pltpu.roll(a, d, axis=0), 1.0)
        u_sh = jnp.where(row >= d, pltpu.roll(u, d, axis=0), 0.0)
        u = a * u_sh + u
        a = a * a_sh
        d *= 2
    first = (i % tiles_per_seq) == 0
    h_prev = jnp.where(first, 0.0, h_ref[...])
    h = u + a * h_prev
    h_ref[...] = h[rows - 1:rows, :]
    o_ref[...] = (h * gate).astype(o_ref.dtype)


def _lru(xg, wg, gb, lam, l, *, seq, tm=256):
    t = xg.shape[0]
    return pl.pallas_call(
        functools.partial(_lru_kernel, seq // tm),
        grid=(t // tm,),
        in_specs=[
            pl.BlockSpec((tm, 2 * LRU_W), lambda i: (i, 0)),
            _pick(wg, l), _pick(gb, l), _pick(lam, l),
        ],
        out_specs=pl.BlockSpec((tm, LRU_W), lambda i: (i, 0)),
        out_shape=jax.ShapeDtypeStruct((t, LRU_W), BF16),
        scratch_shapes=[pltpu.VMEM((1, LRU_W), F32)],
        compiler_params=_params("arbitrary"),
        name="rg_lru",
    )(xg, wg, gb, lam)


_QB = 4 * CHUNK
_KB = 3
_CHUNK_SHIFT = CHUNK.bit_length() - 1


def _band_kernel(q_ref, k0_ref, k1_ref, k2_ref, v0_ref, v1_ref, v2_ref, rel_ref, o_ref, bias_ref):
    i = pl.program_id(1)

    @pl.when((pl.program_id(0) == 0) & (i == 0))
    def _():
        qpos = lax.broadcasted_iota(jnp.int32, (_QB, _QB), 0)
        kcol = lax.broadcasted_iota(jnp.int32, (_QB, _QB), 1)
        for m in range(_KB):
            chunk_off = ((kcol + (m - (_KB - 1)) * _QB) >> _CHUNK_SHIFT) - (qpos >> _CHUNK_SHIFT)
            valid = (chunk_off <= 0) & (chunk_off >= -LA_PAST)
            for h in range(LA_HEADS):
                ev = jnp.broadcast_to(rel_ref[h, m:m + 1, :], (_QB, 2 * _QB))
                toeplitz = pltpu.roll(ev, _QB, axis=1, stride=1, stride_axis=0)[:, :_QB]
                bias_ref[h, m] = jnp.where(valid, toeplitz, NEG_INF)

    k_refs = (k0_ref, k1_ref, k2_ref)
    v_refs = (v0_ref, v1_ref, v2_ref)
    lane = lax.broadcasted_iota(jnp.int32, (1, LANES), 1)
    half_sel = (lane < LA_HD, lane >= LA_HD)

    def attend(ms):
        def scores(h):
            ps = slice((h // 2) * LANES, (h // 2 + 1) * LANES)
            qh = jnp.where(half_sel[h % 2], q_ref[0, :, ps], jnp.zeros((), BF16))
            return [_dot_nt(qh, k_refs[m][0, :, ps]) for m in ms]

        s_next = scores(0)
        pair_out = None
        for h in range(LA_HEADS):
            ps = slice((h // 2) * LANES, (h // 2 + 1) * LANES)
            sel = half_sel[h % 2]
            s = s_next
            if h + 1 < LA_HEADS:
                s_next = scores(h + 1)
            s = [sm + bias_ref[h, m] for sm, m in zip(s, ms)]
            mx = s[0].max(axis=-1, keepdims=True)
            for sm in s[1:]:
                mx = jnp.maximum(mx, sm.max(axis=-1, keepdims=True))
            acc = None
            for sm, m in zip(s, ms):
                vh = jnp.where(sel, v_refs[m][0, :, ps], jnp.ones((), BF16))
                part = _dot(jnp.exp((sm - mx).astype(BF16)), vh)
                acc = part if acc is None else acc + part
            den = pltpu.roll(acc, LA_HD, axis=1)
            out = jnp.where(sel, acc / den, 0.0)
            if h % 2 == 0:
                pair_out = out
            else:
                o_ref[0, :, ps] = (pair_out + out).astype(o_ref.dtype)

    for nvalid in range(1, _KB + 1):
        pl.when(jnp.minimum(i, _KB - 1) == nvalid - 1)(
            functools.partial(attend, tuple(range(_KB - nvalid, _KB))))


def _band_attention(qkv, rel_rows, l, *, batch, seq):
    hw = LA_HEADS * LA_HD
    nblk = seq // _QB

    def kv_spec(col, m):
        return pl.BlockSpec((1, _QB, hw), lambda b, i: (b, jnp.maximum(i - (_KB - 1 - m), 0), col))

    return pl.pallas_call(
        _band_kernel,
        grid=(batch, nblk),
        in_specs=[pl.BlockSpec((1, _QB, hw), lambda b, i: (b, i, 0))]
        + [kv_spec(1, m) for m in range(_KB)]
        + [kv_spec(2, m) for m in range(_KB)]
        + [_pick(rel_rows, l)],
        out_specs=pl.BlockSpec((1, _QB, hw), lambda b, i: (b, i, 0)),
        out_shape=jax.ShapeDtypeStruct((batch, seq, hw), BF16),
        scratch_shapes=[pltpu.VMEM((LA_HEADS, _KB, _QB, _QB), F32)],
        compiler_params=_params("arbitrary", "arbitrary"),
        name="band_attention",
    )(qkv, qkv, qkv, qkv, qkv, qkv, qkv, rel_rows)


def _band_rel_rows(rel_table):
    t = rel_table.astype(F32)
    lo = _KB * _QB - REL_CLIP
    hi = _QB - 1 - REL_CLIP
    full = jnp.concatenate([jnp.broadcast_to(t[..., :1], t.shape[:-1] + (lo,)), t,
                            jnp.broadcast_to(t[..., -1:], t.shape[:-1] + (hi,))], axis=-1)
    return jnp.stack([full[..., m * _QB:(m + 2) * _QB] for m in range(_KB)], axis=-2)


_SUB_TILES = 2


def _sub_tiles(rows):
    step = rows // _SUB_TILES
    return [slice(s * step, (s + 1) * step) for s in range(_SUB_TILES)]


def _mix_out_kernel(ya_ref, yb_ref, yc_ref, xb_ref, x_ref, wgl_ref, wb_ref, wo_ref, g_ref, b_ref,
                    o_ref, ob_ref):
    ys = (ya_ref, yb_ref, yc_ref)
    subs = _sub_tiles(x_ref.shape[0])
    ups = [[_dot(ys[r][rs, :], wb_ref[r]) for r in range(N_BRANCH)] for rs in subs]
    gl = [[_dot(xb_ref[rs, :], wgl_ref[:, r * D_MODEL:(r + 1) * D_MODEL]) for r in range(N_BRANCH)]
          for rs in subs]
    merged = []
    for up, g in zip(ups, gl):
        m = None
        for r in range(N_BRANCH):
            term = _sigmoid(g[r]) * up[r]
            m = term if m is None else m + term
        merged.append(m.astype(BF16))
    y = [_dot(m, wo_ref[...]) for m in merged]
    for rs, yy in zip(subs, y):
        out = _layer_norm(ALPHA * x_ref[rs, :] + yy, g_ref[...], b_ref[...])
        o_ref[rs, :] = out
        ob_ref[rs, :] = out.astype(BF16)


def _mix_out(ya, yb, yc, xb, x, wgl, wb, wo, ln_g, ln_b, l, *, tm=512):
    t, d = x.shape
    return pl.pallas_call(
        _mix_out_kernel,
        grid=(t // tm,),
        in_specs=[
            pl.BlockSpec((tm, BR_W), lambda i: (i, 0)),
            pl.BlockSpec((tm, BR_W), lambda i: (i, 0)),
            pl.BlockSpec((tm, BR_W), lambda i: (i, 0)),
            pl.BlockSpec((tm, d), lambda i: (i, 0)),
            pl.BlockSpec((tm, d), lambda i: (i, 0)),
            _pick(wgl, l), _pick(wb, l), _pick(wo, l), _pick(ln_g, l, 1), _pick(ln_b, l, 1),
        ],
        out_specs=[pl.BlockSpec((tm, d), lambda i: (i, 0)), pl.BlockSpec((tm, d), lambda i: (i, 0))],
        out_shape=[jax.ShapeDtypeStruct((t, d), F32), jax.ShapeDtypeStruct((t, d), BF16)],
        compiler_params=_params("parallel"),
        name="mix_out",
    )(ya, yb, yc, xb, x, wgl, wb, wo, ln_g, ln_b)


def _xattn_kernel(xb_ref, x_ref, kv_ref, wq_ref, wo_ref, g_ref, b_ref, o_ref, ob_ref):
    subs = _sub_tiles(x_ref.shape[1])
    q = [(_dot(xb_ref[0, rs, :], wq_ref[...]) * np.float32(XA_HD ** -0.5)).astype(BF16) for rs in subs]

    def scores(task):
        s, h = task
        sl = slice(h * XA_HD, (h + 1) * XA_HD)
        return _dot_nt(q[s][:, sl], kv_ref[0, :, sl])

    tasks = [(s, h) for h in range(XA_HEADS) for s in range(len(subs))]
    outs = [[None] * XA_HEADS for _ in subs]
    s_next = scores(tasks[0])
    for ti, (s, h) in enumerate(tasks):
        v = kv_ref[0, :, D_MODEL + h * XA_HD:D_MODEL + (h + 1) * XA_HD]
        sc = s_next
        if ti + 1 < len(tasks):
            s_next = scores(tasks[ti + 1])
        p = jnp.exp(sc - sc.max(axis=-1, keepdims=True))
        den = p.sum(axis=-1, keepdims=True)
        outs[s][h] = (_dot(p.astype(BF16), v) / den).astype(BF16)
    y = [_dot(jnp.concatenate(o, axis=1), wo_ref[...]) for o in outs]
    for rs, yy in zip(subs, y):
        out = _layer_norm(ALPHA * x_ref[0, rs, :] + yy, g_ref[...], b_ref[...])
        o_ref[0, rs, :] = out
        ob_ref[0, rs, :] = out.astype(BF16)


def _xattn(xb, x, kv, wq, wo, ln_g, ln_b, l, *, tm=512):
    batch, seq, d = x.shape
    return pl.pallas_call(
        _xattn_kernel,
        grid=(batch, seq // tm),
        in_specs=[
            pl.BlockSpec((1, tm, d), lambda bi, i: (bi, i, 0)),
            pl.BlockSpec((1, tm, d), lambda bi, i: (bi, i, 0)),
            pl.BlockSpec((1, MEM_LEN, 2 * d), lambda bi, i: (bi, 0, 0)),
            _pick(wq, l), _pick(wo, l), _pick(ln_g, l, 2), _pick(ln_b, l, 2),
        ],
        out_specs=[
            pl.BlockSpec((1, tm, d), lambda bi, i: (bi, i, 0)),
            pl.BlockSpec((1, tm, d), lambda bi, i: (bi, i, 0)),
        ],
        out_shape=[jax.ShapeDtypeStruct((batch, seq, d), F32), jax.ShapeDtypeStruct((batch, seq, d), BF16)],
        compiler_params=_params("parallel", "parallel"),
        name="mem_xattn",
    )(xb, x, kv, wq, wo, ln_g, ln_b)


def _prepare(p):
    d = D_MODEL
    depth = p["ln_g"].shape[0]
    w_in = p["mix_w_in"]
    att_scale = jnp.concatenate([jnp.full((LA_HEADS * LA_HD,), LA_HD ** -0.5, F32),
                                 jnp.ones((2 * LA_HEADS * LA_HD,), F32)])
    w_packed = jnp.concatenate([
        w_in[:, :, :_O_DB],
        jnp.pad(w_in[:, :, _O_DB:_O_LX], ((0, 0), (0, 0), (0, LANES - 2 * DN_HEADS))),
        w_in[:, :, _O_LX:_O_AQ],
        w_in[:, :, _O_AQ:_O_GL] * att_scale], axis=2).astype(BF16)

    def lane_row(v):
        return jnp.pad(v.astype(F32), ((0, 0), (DN_HEADS, LANES - 2 * DN_HEADS)))[:, None, :]

    eye = jnp.eye(LRU_BLOCKS, dtype=F32)
    wg = jnp.einsum("lgnde,nm->lgndme", p["lru_gate_w"].astype(F32), eye)
    wg = wg.reshape(depth, 2, LRU_W, LRU_W)
    wg = jnp.concatenate([wg[:, 0], wg[:, 1]], axis=2).astype(BF16)
    return dict(
        ln_g=p["ln_g"].astype(F32).reshape(depth, 4, 1, d),
        ln_b=p["ln_b"].astype(F32).reshape(depth, 4, 1, d),
        w12=p["ffn_w12"].astype(BF16), w3=p["ffn_w3"].astype(BF16),
        w_in=w_packed, wgl=w_in[:, :, _O_GL:].astype(BF16),
        dcw=p["dn_conv_w"].astype(F32), lcw=p["lru_conv_w"].astype(F32),
        lcb=p["lru_conv_b"].astype(F32)[:, None, :],
        alog=lane_row(p["dn_a_log"]), dtb=lane_row(p["dn_dt_bias"]),
        nw=p["dn_norm_w"].astype(F32)[:, None, :],
        wg=wg, gb=p["lru_gate_b"].astype(F32).reshape(depth, 1, 2 * LRU_W),
        lam=p["lru_lambda"].astype(F32)[:, None, :],
        rel=_band_rel_rows(p["la_rel_bias"]),
        wb=p["w_branch"].astype(BF16), wout=p["mix_w_out"].astype(BF16),
        wq=p["xa_wq"].astype(BF16), wkv=p["xa_wkv"].astype(BF16), wo=p["xa_wo"].astype(BF16))


def _layer(x, mem_b, l, w):
    batch, seq, d = x.shape
    t = batch * seq

    x, xb = _ffn(x.reshape(t, d), w["w12"], w["w3"], w["ln_g"], w["ln_b"], (l, 0))

    qkvz, ba, xg, aqkv = _mixer_in(xb, w["w_in"], w["dcw"], w["lcw"], w["lcb"], l, seq=seq)
    qkvz = qkvz.reshape(batch, seq, -1)
    factors = _delta_prep(qkvz, ba.reshape(batch, seq, LANES), w["alog"], w["dtb"], l)
    ya = _delta_scan(*factors, qkvz, w["nw"], l)
    yb = _lru(xg, w["wg"], w["gb"], w["lam"], l, seq=seq)
    yc = _band_attention(aqkv.reshape(batch, seq, -1), w["rel"], l, batch=batch, seq=seq)
    x, xb = _mix_out(ya.reshape(t, -1), yb, yc.reshape(t, -1), xb, x, w["wgl"], w["wb"], w["wout"],
                     w["ln_g"], w["ln_b"], l)

    kv = _proj_plain(mem_b, w["wkv"], l, tn=D_MODEL, out_dtype=BF16, name="proj_kv")
    x, xb = _xattn(xb.reshape(batch, seq, d), x.reshape(batch, seq, d), kv.reshape(batch, MEM_LEN, 2 * d),
                   w["wq"], w["wo"], w["ln_g"], w["ln_b"], l)

    x, _ = _ffn(x.reshape(t, d), w["w12"], w["w3"], w["ln_g"], w["ln_b"], (l, 1))
    return x.reshape(batch, seq, d)


def kernel(x, mem, ln_g, ln_b, ffn_w12, ffn_w3, mix_w_in, dn_conv_w, dn_a_log, dn_dt_bias, dn_norm_w,
           lru_conv_w, lru_conv_b, lru_gate_w, lru_gate_b, lru_lambda, la_rel_bias, w_branch, mix_w_out,
           xa_wq, xa_wkv, xa_wo):
    w = _prepare(dict(
        ln_g=ln_g, ln_b=ln_b, ffn_w12=ffn_w12, ffn_w3=ffn_w3, mix_w_in=mix_w_in, dn_conv_w=dn_conv_w,
        dn_a_log=dn_a_log, dn_dt_bias=dn_dt_bias, dn_norm_w=dn_norm_w, lru_conv_w=lru_conv_w,
        lru_conv_b=lru_conv_b, lru_gate_w=lru_gate_w, lru_gate_b=lru_gate_b, lru_lambda=lru_lambda,
        la_rel_bias=la_rel_bias, w_branch=w_branch, mix_w_out=mix_w_out, xa_wq=xa_wq, xa_wkv=xa_wkv,
        xa_wo=xa_wo))
    batch = x.shape[0]
    mem_b = mem.reshape(batch * MEM_LEN, D_MODEL).astype(BF16)
    x = x.astype(F32)
    for l in range(DEPTH):
        x = _layer(x, mem_b, l, w)
    return x
```

```python
import functools

import numpy as np
import jax
import jax.numpy as jnp
from jax import lax
from jax.experimental import pallas as pl
from jax.experimental.pallas import tpu as pltpu

F32 = jnp.float32
BF16 = jnp.bfloat16

D_MODEL = 1024
DEPTH = 2
CHUNK = 64
CONV_W = 4
BR_W = 512
N_BRANCH = 3
DN_HEADS = 4
DN_DK = 128
DN_DV = 128
LRU_W = BR_W
LRU_BLOCKS = 8
LRU_BLK = LRU_W // LRU_BLOCKS
LRU_C = 8.0
LA_HEADS = 8
LA_HD = 64
LA_PAST = 8
REL_CLIP = 128
MEM_LEN = 256
XA_HEADS = 4
XA_HD = D_MODEL // XA_HEADS
D_FF = 2816
ALPHA = (2 * DEPTH) ** 0.25
LN_EPS = 1e-5
NORM_EPS = 1e-6
NEG_INF = -1e30

LANES = 128
SUBLANES = 8
VMEM_LIMIT = 56 * 1024 * 1024

_O_DZ = 3 * DN_HEADS * DN_DK
_O_DB = _O_DZ + DN_HEADS * DN_DV
_O_LX = _O_DB + 2 * DN_HEADS
_O_AQ = _O_LX + 2 * LRU_W
_O_GL = _O_AQ + 3 * LA_HEADS * LA_HD


def _params(*sem):
    return pltpu.CompilerParams(dimension_semantics=sem, vmem_limit_bytes=VMEM_LIMIT)


def _dot(a, b):
    return jnp.dot(a, b, preferred_element_type=F32)


def _dot_nt(a, b):
    return lax.dot_general(a, b, (((1,), (1,)), ((), ())), preferred_element_type=F32)


def _sigmoid(x):
    return 1.0 / (1.0 + jnp.exp(-x))


def _softplus(x):
    return jnp.maximum(x, 0.0) + jnp.log1p(jnp.exp(-jnp.abs(x)))


def _gelu_tanh(x):
    c = np.float32(np.sqrt(2.0 / np.pi))
    return 0.5 * x * (1.0 + jnp.tanh(c * (x + np.float32(0.044715) * (x * x * x))))


def _layer_norm(y, g, b):
    mu = jnp.mean(y, axis=-1, keepdims=True)
    d = y - mu
    var = jnp.mean(d * d, axis=-1, keepdims=True)
    return d * lax.rsqrt(var + LN_EPS) * g + b


def _pick(arr, *idx):
    shape = (None,) * len(idx) + tuple(arr.shape[len(idx):])
    index = tuple(idx) + (0,) * (arr.ndim - len(idx))
    return pl.BlockSpec(shape, lambda *_: index, pipeline_mode=pl.Buffered(1))


_FF_CHUNK = 256


def _ffn_kernel(x_ref, w12_ref, w3_ref, g_ref, b_ref, o_ref, ob_ref):
    x = x_ref[...]
    xb = x.astype(BF16)
    n = D_FF // _FF_CHUNK

    def gate_up(c):
        lo = c * _FF_CHUNK
        return (_dot(xb, w12_ref[:, lo:lo + _FF_CHUNK]),
                _dot(xb, w12_ref[:, D_FF + lo:D_FF + lo + _FF_CHUNK]))

    nxt = gate_up(0)
    acc = None
    for c in range(n):
        g, u = nxt
        if c + 1 < n:
            nxt = gate_up(c + 1)
        h = (g * _sigmoid(g) * u).astype(BF16)
        part = _dot(h, w3_ref[c * _FF_CHUNK:(c + 1) * _FF_CHUNK, :])
        acc = part if acc is None else acc + part
    out = _layer_norm(ALPHA * x + 0.5 * acc, g_ref[...], b_ref[...])
    o_ref[...] = out
    ob_ref[...] = out.astype(BF16)


def _ffn(x, w12, w3, ln_g, ln_b, sel, *, tm=512):
    t, d = x.shape
    l, k = sel
    return pl.pallas_call(
        _ffn_kernel,
        grid=(t // tm,),
        in_specs=[
            pl.BlockSpec((tm, d), lambda i: (i, 0)),
            _pick(w12, l, k),
            _pick(w3, l, k),
            _pick(ln_g, l, 3 * k),
            _pick(ln_b, l, 3 * k),
        ],
        out_specs=[
            pl.BlockSpec((tm, d), lambda i: (i, 0)),
            pl.BlockSpec((tm, d), lambda i: (i, 0)),
        ],
        out_shape=[jax.ShapeDtypeStruct((t, d), F32), jax.ShapeDtypeStruct((t, d), BF16)],
        compiler_params=_params("parallel"),
        name="ffn_ln",
    )(x, w12, w3, ln_g, ln_b)


_HW = DN_HEADS * DN_DK
_C_QKV = 0
_C_Z = 3 * _HW
_C_BA = _C_Z + _HW
_C_LX = _C_BA + LANES
_C_LG = _C_LX + LRU_W
_C_ATT = _C_LG + LRU_W
_SEG = 256
_N_TAILS = (3 * _HW + LRU_W) // _SEG


def _causal_conv(y, tail, cw):
    row8 = lax.broadcasted_iota(jnp.int32, (SUBLANES, y.shape[1]), 0)
    acc = y * cw[CONV_W - 1:CONV_W, :]
    fix = jnp.zeros((SUBLANES, y.shape[1]), F32)
    for k in range(1, CONV_W):
        wk = cw[CONV_W - 1 - k:CONV_W - k, :]
        sh = pltpu.roll(y, k, axis=0)
        acc = acc + sh * wk
        prev = pltpu.roll(tail, k, axis=0)
        fix = fix + jnp.where(row8 < k, (prev - sh[:SUBLANES]) * wk, 0.0)
    return jnp.concatenate([acc[:SUBLANES] + fix, acc[SUBLANES:]], axis=0)


def _mixer_in_kernel(tiles_per_seq, x_ref, w_ref, dcw_ref, lcw_ref, lcb_ref,
                     qkvz_ref, ba_ref, xg_ref, att_ref, tail_ref):
    xb = x_ref[...]
    first = (pl.program_id(0) % tiles_per_seq) == 0
    rows = xb.shape[0]

    def proj(c0, width=_SEG):
        return _dot(xb, w_ref[:, c0:c0 + width])

    def cols(j):
        return slice(j * _SEG, (j + 1) * _SEG)

    def conv(y, slot, cw):
        tail = jnp.where(first, 0.0, tail_ref[slot])
        tail_ref[slot] = y[rows - SUBLANES:]
        return _causal_conv(y, tail, cw)

    def delta_qkv(j):
        c = conv(proj(_C_QKV + j * _SEG), j, dcw_ref[:, cols(j)])
        c = c * _sigmoid(c)
        if j * _SEG < 2 * _HW:
            scale = np.float32(DN_DK ** -0.5 if j * _SEG < _HW else 1.0)
            parts = []
            for h in range(_SEG // DN_DK):
                ch = c[:, h * DN_DK:(h + 1) * DN_DK]
                parts.append(ch * (lax.rsqrt(jnp.sum(ch * ch, axis=-1, keepdims=True) + NORM_EPS) * scale))
            c = jnp.concatenate(parts, axis=1)
        qkvz_ref[:, cols(j)] = c.astype(BF16)

    def z_part(j):
        qkvz_ref[:, 3 * _HW + j * _SEG:3 * _HW + (j + 1) * _SEG] = proj(_C_Z + j * _SEG).astype(BF16)

    def lru_x(j):
        slot = 3 * _HW // _SEG + j
        xg_ref[:, cols(j)] = (conv(proj(_C_LX + j * _SEG), slot, lcw_ref[:, cols(j)])
                              + lcb_ref[:, cols(j)]).astype(BF16)

    def lru_g(j):
        xg_ref[:, LRU_W + j * _SEG:LRU_W + (j + 1) * _SEG] = _gelu_tanh(proj(_C_LG + j * _SEG)).astype(BF16)

    def att(j):
        att_ref[:, cols(j)] = proj(_C_ATT + j * _SEG).astype(BF16)

    heavy = ([functools.partial(delta_qkv, j) for j in range(3 * _HW // _SEG)]
             + [functools.partial(lru_x, j) for j in range(LRU_W // _SEG)]
             + [functools.partial(lru_g, j) for j in range(LRU_W // _SEG)])
    light = ([functools.partial(att, j) for j in range(3 * _HW // _SEG)]
             + [functools.partial(z_part, j) for j in range(_HW // _SEG)])
    for i, fn in enumerate(heavy):
        fn()
        if i < len(light):
            light[i]()
    for g in light[len(heavy):]:
        g()
    ba_ref[...] = proj(_C_BA, LANES)


def _mixer_in(xb, w, dcw, lcw, lcb, l, *, seq, tm=512):
    t, d = xb.shape

    def tile(width):
        return pl.BlockSpec((tm, width), lambda i: (i, 0))

    widths = (4 * _HW, LANES, 2 * LRU_W, 3 * _HW)
    dtypes = (BF16, F32, BF16, BF16)
    return pl.pallas_call(
        functools.partial(_mixer_in_kernel, seq // tm),
        grid=(t // tm,),
        in_specs=[tile(d), _pick(w, l), _pick(dcw, l), _pick(lcw, l), _pick(lcb, l)],
        out_specs=[tile(wd) for wd in widths],
        out_shape=[jax.ShapeDtypeStruct((t, wd), dt) for wd, dt in zip(widths, dtypes)],
        scratch_shapes=[pltpu.VMEM((_N_TAILS, SUBLANES, _SEG), F32)],
        compiler_params=_params("arbitrary"),
        name="mixer_in",
    )(xb, w, dcw, lcw, lcb)


def _proj_plain_kernel(x_ref, w_ref, o_ref):
    o_ref[...] = _dot(x_ref[...], w_ref[...]).astype(o_ref.dtype)


def _proj_plain(xb, w, l, *, tn, out_dtype, name):
    t, d = xb.shape
    n = w.shape[2]
    return pl.pallas_call(
        _proj_plain_kernel,
        grid=(n // tn,),
        in_specs=[pl.BlockSpec((t, d), lambda j: (0, 0)), pl.BlockSpec((None, d, tn), lambda j: (l, 0, j))],
        out_specs=pl.BlockSpec((t, tn), lambda j: (0, j)),
        out_shape=jax.ShapeDtypeStruct((t, n), out_dtype),
        compiler_params=_params("parallel"),
        name=name,
    )(xb, w)


_PREP_CHUNKS = 8
_SCAN_CHUNKS = 4


def _segmented_cumsum_rows(x, seg):
    row = lax.broadcasted_iota(jnp.int32, x.shape, 0) & (seg - 1)
    d = 1
    while d < seg:
        x = x + jnp.where(row >= d, pltpu.roll(x, d, axis=0), 0.0)
        d *= 2
    return x


def _delta_prep_kernel(q_ref, k_ref, v_ref, ba_ref, alog_ref, dtb_ref,
                       wq_ref, u_ref, qk_ref, kdt_ref, gl_ref):
    row = lax.broadcasted_iota(jnp.int32, (CHUNK, CHUNK), 0)
    col = lax.broadcasted_iota(jnp.int32, (CHUNK, CHUNK), 1)
    lower = row >= col
    strict = row > col
    eye = jnp.where(row == col, 1.0, 0.0).astype(F32)
    zpad = jnp.zeros((LANES - CHUNK, LANES), F32)

    ba = ba_ref[0]
    beta_all = _sigmoid(ba)
    g_all = -jnp.exp(alog_ref[...]) * _softplus(ba + dtb_ref[...])
    g_cum = _segmented_cumsum_rows(g_all, CHUNK)
    g_cum_t = [jnp.transpose(g_cum[pr * LANES:(pr + 1) * LANES]) for pr in range(_PREP_CHUNKS // 2)]

    inst = [(ci, h) for ci in range(_PREP_CHUNKS) for h in range(DN_HEADS)]
    st = []
    for ci, h in inst:
        rs = slice(ci * CHUNK, (ci + 1) * CHUNK)
        hs = slice(h * DN_DK, (h + 1) * DN_DK)
        q = q_ref[0, rs, hs].astype(F32)
        k = k_ref[0, rs, hs].astype(F32)
        beta = beta_all[rs, h:h + 1]
        gc = g_cum[rs, DN_HEADS + h:DN_HEADS + h + 1]
        gr = g_cum_t[ci // 2][DN_HEADS + h:DN_HEADS + h + 1, (ci % 2) * CHUNK:(ci % 2 + 1) * CHUNK]
        decay = jnp.exp(jnp.where(lower, gc - gr, NEG_INF))
        kb = k * beta
        aq = _dot_nt(jnp.concatenate([kb, q], axis=0).astype(BF16), k.astype(BF16))
        st.append(dict(rs=rs, hs=hs, beta=beta, gc=gc, kb=kb, aq=aq, decay=decay))

    for s in st:
        s["p"] = -jnp.where(strict, s["aq"][:CHUNK] * s["decay"], 0.0)
        s["tinv"] = eye + s["p"]
    n = 1
    while 2 * n < CHUNK:
        for s in st:
            p16 = s["p"].astype(BF16)
            s["p"] = _dot(p16, p16)
        for s in st:
            s["tinv"] = s["tinv"] + _dot(s["tinv"].astype(BF16), s["p"].astype(BF16))
        n *= 2

    for s in st:
        exp_g = jnp.exp(s["gc"])
        v = v_ref[0, s["rs"], s["hs"]].astype(F32)
        rhs = jnp.concatenate([v * s["beta"], s["kb"] * exp_g], axis=1)
        s["sol"] = _dot(s["tinv"].astype(BF16), rhs.astype(BF16))
        s["exp_g"] = exp_g

    for (ci, h), s in zip(inst, st):
        q = q_ref[0, s["rs"], s["hs"]].astype(F32)
        k = k_ref[0, s["rs"], s["hs"]].astype(F32)
        g_last = s["gc"][CHUNK - 1:CHUNK, :]
        k_dec = k * jnp.exp(g_last - s["gc"])
        kdt = jnp.transpose(jnp.concatenate([k_dec, zpad], axis=0))[:, :CHUNK]
        wq_ref[0, ci, h * 2 * CHUNK:(h + 1) * 2 * CHUNK, :] = jnp.concatenate(
            [s["sol"][:, DN_DV:], q * s["exp_g"]], axis=0).astype(BF16)
        u_ref[0, ci, h * CHUNK:(h + 1) * CHUNK, :] = s["sol"][:, :DN_DV]
        qk_ref[0, ci, h * CHUNK:(h + 1) * CHUNK, :] = (s["aq"][CHUNK:] * s["decay"]).astype(BF16)
        kdt_ref[0, ci, h * DN_DK:(h + 1) * DN_DK, :] = kdt.astype(BF16)
        gl_ref[0, ci, h * SUBLANES:(h + 1) * SUBLANES, :] = jnp.broadcast_to(
            jnp.exp(g_last), (SUBLANES, LANES))


def _delta_prep(qkvz, ba, alog, dtb, l):
    batch, seq, _ = qkvz.shape
    n = seq // CHUNK
    rows = _PREP_CHUNKS * CHUNK
    hw = DN_HEADS * DN_DK

    def col_spec(cb):
        return pl.BlockSpec((1, rows, hw), lambda b, i: (b, i, cb))

    def out_spec(r, c):
        return pl.BlockSpec((1, _PREP_CHUNKS, r, c), lambda b, i: (b, i, 0, 0))

    return pl.pallas_call(
        _delta_prep_kernel,
        grid=(batch, n // _PREP_CHUNKS),
        in_specs=[col_spec(0), col_spec(1), col_spec(2),
                  pl.BlockSpec((1, rows, LANES), lambda b, i: (b, i, 0)),
                  _pick(alog, l), _pick(dtb, l)],
        out_specs=[out_spec(DN_HEADS * 2 * CHUNK, DN_DK), out_spec(DN_HEADS * CHUNK, DN_DV),
                   out_spec(DN_HEADS * CHUNK, CHUNK), out_spec(DN_HEADS * DN_DK, CHUNK),
                   out_spec(DN_HEADS * SUBLANES, LANES)],
        out_shape=[jax.ShapeDtypeStruct((batch, n, DN_HEADS * 2 * CHUNK, DN_DK), BF16),
                   jax.ShapeDtypeStruct((batch, n, DN_HEADS * CHUNK, DN_DV), F32),
                   jax.ShapeDtypeStruct((batch, n, DN_HEADS * CHUNK, CHUNK), BF16),
                   jax.ShapeDtypeStruct((batch, n, DN_HEADS * DN_DK, CHUNK), BF16),
                   jax.ShapeDtypeStruct((batch, n, DN_HEADS * SUBLANES, LANES), F32)],
        compiler_params=_params("parallel", "parallel"),
        name="delta_prep",
    )(qkvz, qkvz, qkvz, ba, alog, dtb)


def _delta_scan_kernel(batch, wq_ref, u_ref, qk_ref, kdt_ref, gl_ref, z_ref, nw_ref, o_ref, state_ref):
    c = pl.program_id(0)

    @pl.when(c == 0)
    def _():
        state_ref[...] = jnp.zeros_like(state_ref)

    chains = [(b, h) for b in range(batch) for h in range(DN_HEADS)]
    for cc in range(_SCAN_CHUNKS):
        rs = slice(cc * CHUNK, (cc + 1) * CHUNK)
        r = [_dot(wq_ref[b, cc, h * 2 * CHUNK:(h + 1) * 2 * CHUNK, :],
                  state_ref[b * DN_HEADS + h].astype(BF16)) for b, h in chains]
        v_new = [(u_ref[b, cc, h * CHUNK:(h + 1) * CHUNK, :] - r[i][:CHUNK]).astype(BF16)
                 for i, (b, h) in enumerate(chains)]
        kv = [_dot(kdt_ref[b, cc, h * DN_DK:(h + 1) * DN_DK, :], v_new[i]) for i, (b, h) in enumerate(chains)]
        qv = [_dot(qk_ref[b, cc, h * CHUNK:(h + 1) * CHUNK, :], v_new[i]) for i, (b, h) in enumerate(chains)]
        for i, (b, h) in enumerate(chains):
            state_ref[b * DN_HEADS + h] = (
                state_ref[b * DN_HEADS + h] * gl_ref[b, cc, h * SUBLANES:h * SUBLANES + 1, :] + kv[i])
            o = r[i][CHUNK:] + qv[i]
            z = z_ref[b, rs, h * DN_DV:(h + 1) * DN_DV].astype(F32)
            o = o * lax.rsqrt(jnp.mean(o * o, axis=-1, keepdims=True) + NORM_EPS)
            o = o * nw_ref[...] * (z * _sigmoid(z))
            o_ref[b, rs, h * DN_DV:(h + 1) * DN_DV] = o.astype(o_ref.dtype)


def _delta_scan(wq, u, qk, kdt, gl, qkvz, nw, l):
    batch, seq, _ = qkvz.shape
    hw = DN_HEADS * DN_DV

    rows = _SCAN_CHUNKS * CHUNK

    def step_spec(a):
        return pl.BlockSpec((batch, _SCAN_CHUNKS) + a.shape[2:], lambda c: (0, c, 0, 0))

    return pl.pallas_call(
        functools.partial(_delta_scan_kernel, batch),
        grid=(seq // rows,),
        in_specs=[step_spec(wq), step_spec(u), step_spec(qk), step_spec(kdt), step_spec(gl),
                  pl.BlockSpec((batch, rows, hw), lambda c: (0, c, 3)),
                  _pick(nw, l)],
        out_specs=pl.BlockSpec((batch, rows, hw), lambda c: (0, c, 0)),
        out_shape=jax.ShapeDtypeStruct((batch, seq, hw), BF16),
        scratch_shapes=[pltpu.VMEM((batch * DN_HEADS, DN_DK, DN_DV), F32)],
        compiler_params=_params("arbitrary"),
        name="delta_scan",
    )(wq, u, qk, kdt, gl, qkvz, nw)


def _lru_kernel(tiles_per_seq, xg_ref, wg_ref, gb_ref, lam_ref, o_ref, h_ref):
    i = pl.program_id(0)
    rows = xg_ref.shape[0]
    xc = xg_ref[:, :LRU_W].astype(F32)
    gate = xg_ref[:, LRU_W:].astype(F32)
    gates = _dot(xg_ref[:, :LRU_W], wg_ref[...]) + gb_ref[...]
    r = _sigmoid(gates[:, :LRU_W])
    ig = _sigmoid(gates[:, LRU_W:])
    log_a = -LRU_C * r * _softplus(-lam_ref[...])
    a = jnp.exp(log_a)
    u = xc * ig * jnp.sqrt(-jnp.tanh(log_a) * (1.0 + a * a))
    row = lax.broadcasted_iota(jnp.int32, a.shape, 0)
    d = 1
    while d < rows:
        a_sh = jnp.where(row >= d, pltpu.roll(a, d, axis=0), 1.0)
        u_sh = jnp.where(row >= d, pltpu.roll(u, d, axis=0), 0.0)
        u = a * u_sh + u
        a = a * a_sh
        d *= 2
    first = (i % tiles_per_seq) == 0
    h_prev = jnp.where(first, 0.0, h_ref[...])
    h = u + a * h_prev
    h_ref[...] = h[rows - 1:rows, :]
    o_ref[...] = (h * gate).astype(o_ref.dtype)


def _lru(xg, wg, gb, lam, l, *, seq, tm=256):
    t = xg.shape[0]
    return pl.pallas_call(
        functools.partial(_lru_kernel, seq // tm),
        grid=(t // tm,),
        in_specs=[
            pl.BlockSpec((tm, 2 * LRU_W), lambda i: (i, 0)),
            _pick(wg, l), _pick(gb, l), _pick(lam, l),
        ],
        out_specs=pl.BlockSpec((tm, LRU_W), lambda i: (i, 0)),
        out_shape=jax.ShapeDtypeStruct((t, LRU_W), BF16),
        scratch_shapes=[pltpu.VMEM((1, LRU_W), F32)],
        compiler_params=_params("arbitrary"),
        name="rg_lru",
    )(xg, wg, gb, lam)


_QB = 4 * CHUNK
_KB = 3
_CHUNK_SHIFT = CHUNK.bit_length() - 1


def _band_kernel(q_ref, k0_ref, k1_ref, k2_ref, v0_ref, v1_ref, v2_ref, rel_ref, o_ref, bias_ref):
    i = pl.program_id(1)

    @pl.when((pl.program_id(0) == 0) & (i == 0))
    def _():
        qpos = lax.broadcasted_iota(jnp.int32, (_QB, _QB), 0)
        kcol = lax.broadcasted_iota(jnp.int32, (_QB, _QB), 1)
        for m in range(_KB):
            chunk_off = ((kcol + (m - (_KB - 1)) * _QB) >> _CHUNK_SHIFT) - (qpos >> _CHUNK_SHIFT)
            valid = (chunk_off <= 0) & (chunk_off >= -LA_PAST)
            for h in range(LA_HEADS):
                ev = jnp.broadcast_to(rel_ref[h, m:m + 1, :], (_QB, 2 * _QB))
                toeplitz = pltpu.roll(ev, _QB, axis=1, stride=1, stride_axis=0)[:, :_QB]
                bias_ref[h, m] = jnp.where(valid, toeplitz, NEG_INF)

    k_refs = (k0_ref, k1_ref, k2_ref)
    v_refs = (v0_ref, v1_ref, v2_ref)
    lane = lax.broadcasted_iota(jnp.int32, (1, LANES), 1)
    half_sel = (lane < LA_HD, lane >= LA_HD)

    def attend(ms):
        def scores(h):
            ps = slice((h // 2) * LANES, (h // 2 + 1) * LANES)
            qh = jnp.where(half_sel[h % 2], q_ref[0, :, ps], jnp.zeros((), BF16))
            return [_dot_nt(qh, k_refs[m][0, :, ps]) for m in ms]

        s_next = scores(0)
        pair_out = None
        for h in range(LA_HEADS):
            ps = slice((h // 2) * LANES, (h // 2 + 1) * LANES)
            sel = half_sel[h % 2]
            s = s_next
            if h + 1 < LA_HEADS:
                s_next = scores(h + 1)
            s = [sm + bias_ref[h, m] for sm, m in zip(s, ms)]
            mx = s[0].max(axis=-1, keepdims=True)
            for sm in s[1:]:
                mx = jnp.maximum(mx, sm.max(axis=-1, keepdims=True))
            acc = None
            for sm, m in zip(s, ms):
                vh = jnp.where(sel, v_refs[m][0, :, ps], jnp.ones((), BF16))
                part = _dot(jnp.exp((sm - mx).astype(BF16)), vh)
                acc = part if acc is None else acc + part
            den = pltpu.roll(acc, LA_HD, axis=1)
            out = jnp.where(sel, acc / den, 0.0)
            if h % 2 == 0:
                pair_out = out
            else:
                o_ref[0, :, ps] = (pair_out + out).astype(o_ref.dtype)

    for nvalid in range(1, _KB + 1):
        pl.when(jnp.minimum(i, _KB - 1) == nvalid - 1)(
            functools.partial(attend, tuple(range(_KB - nvalid, _KB))))


def _band_attention(qkv, rel_rows, l, *, batch, seq):
    hw = LA_HEADS * LA_HD
    nblk = seq // _QB

    def kv_spec(col, m):
        return pl.BlockSpec((1, _QB, hw), lambda b, i: (b, jnp.maximum(i - (_KB - 1 - m), 0), col))

    return pl.pallas_call(
        _band_kernel,
        grid=(batch, nblk),
        in_specs=[pl.BlockSpec((1, _QB, hw), lambda b, i: (b, i, 0))]
        + [kv_spec(1, m) for m in range(_KB)]
        + [kv_spec(2, m) for m in range(_KB)]
        + [_pick(rel_rows, l)],
        out_specs=pl.BlockSpec((1, _QB, hw), lambda b, i: (b, i, 0)),
        out_shape=jax.ShapeDtypeStruct((batch, seq, hw), BF16),
        scratch_shapes=[pltpu.VMEM((LA_HEADS, _KB, _QB, _QB), F32)],
        compiler_params=_params("arbitrary", "arbitrary"),
        name="band_attention",
    )(qkv, qkv, qkv, qkv, qkv, qkv, qkv, rel_rows)


def _band_rel_rows(rel_table):
    t = rel_table.astype(F32)
    lo = _KB * _QB - REL_CLIP
    hi = _QB - 1 - REL_CLIP
    full = jnp.concatenate([jnp.broadcast_to(t[..., :1], t.shape[:-1] + (lo,)), t,
                            jnp.broadcast_to(t[..., -1:], t.shape[:-1] + (hi,))], axis=-1)
    return jnp.stack([full[..., m * _QB:(m + 2) * _QB] for m in range(_KB)], axis=-2)


_SUB_TILES = 2


def _sub_tiles(rows):
    step = rows // _SUB_TILES
    return [slice(s * step, (s + 1) * step) for s in range(_SUB_TILES)]


def _mix_out_kernel(ya_ref, yb_ref, yc_ref, xb_ref, x_ref, wgl_ref, wb_ref, wo_ref, g_ref, b_ref,
                    o_ref, ob_ref):
    ys = (ya_ref, yb_ref, yc_ref)
    subs = _sub_tiles(x_ref.shape[0])
    ups = [[_dot(ys[r][rs, :], wb_ref[r]) for r in range(N_BRANCH)] for rs in subs]
    gl = [[_dot(xb_ref[rs, :], wgl_ref[:, r * D_MODEL:(r + 1) * D_MODEL]) for r in range(N_BRANCH)]
          for rs in subs]
    merged = []
    for up, g in zip(ups, gl):
        m = None
        for r in range(N_BRANCH):
            term = _sigmoid(g[r]) * up[r]
            m = term if m is None else m + term
        merged.append(m.astype(BF16))
    y = [_dot(m, wo_ref[...]) for m in merged]
    for rs, yy in zip(subs, y):
        out = _layer_norm(ALPHA * x_ref[rs, :] + yy, g_ref[...], b_ref[...])
        o_ref[rs, :] = out
        ob_ref[rs, :] = out.astype(BF16)


def _mix_out(ya, yb, yc, xb, x, wgl, wb, wo, ln_g, ln_b, l, *, tm=512):
    t, d = x.shape
    return pl.pallas_call(
        _mix_out_kernel,
        grid=(t // tm,),
        in_specs=[
            pl.BlockSpec((tm, BR_W), lambda i: (i, 0)),
            pl.BlockSpec((tm, BR_W), lambda i: (i, 0)),
            pl.BlockSpec((tm, BR_W), lambda i: (i, 0)),
            pl.BlockSpec((tm, d), lambda i: (i, 0)),
            pl.BlockSpec((tm, d), lambda i: (i, 0)),
            _pick(wgl, l), _pick(wb, l), _pick(wo, l), _pick(ln_g, l, 1), _pick(ln_b, l, 1),
        ],
        out_specs=[pl.BlockSpec((tm, d), lambda i: (i, 0)), pl.BlockSpec((tm, d), lambda i: (i, 0))],
        out_shape=[jax.ShapeDtypeStruct((t, d), F32), jax.ShapeDtypeStruct((t, d), BF16)],
        compiler_params=_params("parallel"),
        name="mix_out",
    )(ya, yb, yc, xb, x, wgl, wb, wo, ln_g, ln_b)


def _xattn_kernel(xb_ref, x_ref, kv_ref, wq_ref, wo_ref, g_ref, b_ref, o_ref, ob_ref):
    subs = _sub_tiles(x_ref.shape[1])
    q = [(_dot(xb_ref[0, rs, :], wq_ref[...]) * np.float32(XA_HD ** -0.5)).astype(BF16) for rs in subs]

    def scores(task):
        s, h = task
        sl = slice(h * XA_HD, (h + 1) * XA_HD)
        return _dot_nt(q[s][:, sl], kv_ref[0, :, sl])

    tasks = [(s, h) for h in range(XA_HEADS) for s in range(len(subs))]
    outs = [[None] * XA_HEADS for _ in subs]
    s_next = scores(tasks[0])
    for ti, (s, h) in enumerate(tasks):
        v = kv_ref[0, :, D_MODEL + h * XA_HD:D_MODEL + (h + 1) * XA_HD]
        sc = s_next
        if ti + 1 < len(tasks):
            s_next = scores(tasks[ti + 1])
        p = jnp.exp(sc - sc.max(axis=-1, keepdims=True))
        den = p.sum(axis=-1, keepdims=True)
        outs[s][h] = (_dot(p.astype(BF16), v) / den).astype(BF16)
    y = [_dot(jnp.concatenate(o, axis=1), wo_ref[...]) for o in outs]
    for rs, yy in zip(subs, y):
        out = _layer_norm(ALPHA * x_ref[0, rs, :] + yy, g_ref[...], b_ref[...])
        o_ref[0, rs, :] = out
        ob_ref[0, rs, :] = out.astype(BF16)


def _xattn(xb, x, kv, wq, wo, ln_g, ln_b, l, *, tm=512):
    batch, seq, d = x.shape
    return pl.pallas_call(
        _xattn_kernel,
        grid=(batch, seq // tm),
        in_specs=[
            pl.BlockSpec((1, tm, d), lambda bi, i: (bi, i, 0)),
            pl.BlockSpec((1, tm, d), lambda bi, i: (bi, i, 0)),
            pl.BlockSpec((1, MEM_LEN, 2 * d), lambda bi, i: (bi, 0, 0)),
            _pick(wq, l), _pick(wo, l), _pick(ln_g, l, 2), _pick(ln_b, l, 2),
        ],
        out_specs=[
            pl.BlockSpec((1, tm, d), lambda bi, i: (bi, i, 0)),
            pl.BlockSpec((1, tm, d), lambda bi, i: (bi, i, 0)),
        ],
        out_shape=[jax.ShapeDtypeStruct((batch, seq, d), F32), jax.ShapeDtypeStruct((batch, seq, d), BF16)],
        compiler_params=_params("parallel", "parallel"),
        name="mem_xattn",
    )(xb, x, kv, wq, wo, ln_g, ln_b)


def _repack_kernel(w_ref, o_ref, g_ref):
    hw = LA_HEADS * LA_HD
    o_ref[:, :_O_DB] = w_ref[:, :_O_DB].astype(BF16)
    o_ref[:, _C_BA:_C_LX] = jnp.zeros((o_ref.shape[0], LANES), BF16)
    o_ref[:, _C_BA:_C_BA + 2 * DN_HEADS] = w_ref[:, _O_DB:_O_LX].astype(BF16)
    o_ref[:, _C_LX:_C_ATT] = w_ref[:, _O_LX:_O_AQ].astype(BF16)
    o_ref[:, _C_ATT:_C_ATT + hw] = (w_ref[:, _O_AQ:_O_AQ + hw] * np.float32(LA_HD ** -0.5)).astype(BF16)
    o_ref[:, _C_ATT + hw:] = w_ref[:, _O_AQ + hw:_O_GL].astype(BF16)
    g_ref[...] = w_ref[:, _O_GL:].astype(BF16)


def _repack_w_in(w_in, *, tr=128):
    depth, d, n_in = w_in.shape
    n_packed = _C_ATT + 3 * LA_HEADS * LA_HD
    n_gate = N_BRANCH * D_MODEL

    def rows(width):
        return pl.BlockSpec((None, tr, width), lambda l, i: (l, i, 0))

    return pl.pallas_call(
        _repack_kernel,
        grid=(depth, d // tr),
        in_specs=[rows(n_in)],
        out_specs=[rows(n_packed), rows(n_gate)],
        out_shape=[jax.ShapeDtypeStruct((depth, d, n_packed), BF16),
                   jax.ShapeDtypeStruct((depth, d, n_gate), BF16)],
        compiler_params=_params("parallel", "parallel"),
        name="repack_w_in",
    )(w_in)


def _prepare(p):
    d = D_MODEL
    depth = p["ln_g"].shape[0]
    w_packed, wgl = _repack_w_in(p["mix_w_in"].astype(F32))

    def lane_row(v):
        return jnp.pad(v.astype(F32), ((0, 0), (DN_HEADS, LANES - 2 * DN_HEADS)))[:, None, :]

    eye = jnp.eye(LRU_BLOCKS, dtype=F32)
    wg = jnp.einsum("lgnde,nm->lgndme", p["lru_gate_w"].astype(F32), eye)
    wg = wg.reshape(depth, 2, LRU_W, LRU_W)
    wg = jnp.concatenate([wg[:, 0], wg[:, 1]], axis=2).astype(BF16)
    return dict(
        ln_g=p["ln_g"].astype(F32).reshape(depth, 4, 1, d),
        ln_b=p["ln_b"].astype(F32).reshape(depth, 4, 1, d),
        w12=p["ffn_w12"].astype(BF16), w3=p["ffn_w3"].astype(BF16),
        w_in=w_packed, wgl=wgl,
        dcw=p["dn_conv_w"].astype(F32), lcw=p["lru_conv_w"].astype(F32),
        lcb=p["lru_conv_b"].astype(F32)[:, None, :],
        alog=lane_row(p["dn_a_log"]), dtb=lane_row(p["dn_dt_bias"]),
        nw=p["dn_norm_w"].astype(F32)[:, None, :],
        wg=wg, gb=p["lru_gate_b"].astype(F32).reshape(depth, 1, 2 * LRU_W),
        lam=p["lru_lambda"].astype(F32)[:, None, :],
        rel=_band_rel_rows(p["la_rel_bias"]),
        wb=p["w_branch"].astype(BF16), wout=p["mix_w_out"].astype(BF16),
        wq=p["xa_wq"].astype(BF16), wkv=p["xa_wkv"].astype(BF16), wo=p["xa_wo"].astype(BF16))


def _layer(x, mem_b, l, w):
    batch, seq, d = x.shape
    t = batch * seq

    x, xb = _ffn(x.reshape(t, d), w["w12"], w["w3"], w["ln_g"], w["ln_b"], (l, 0))

    qkvz, ba, xg, aqkv = _mixer_in(xb, w["w_in"], w["dcw"], w["lcw"], w["lcb"], l, seq=seq)
    qkvz = qkvz.reshape(batch, seq, -1)
    factors = _delta_prep(qkvz, ba.reshape(batch, seq, LANES), w["alog"], w["dtb"], l)
    ya = _delta_scan(*factors, qkvz, w["nw"], l)
    yb = _lru(xg, w["wg"], w["gb"], w["lam"], l, seq=seq)
    yc = _band_attention(aqkv.reshape(batch, seq, -1), w["rel"], l, batch=batch, seq=seq)
    x, xb = _mix_out(ya.reshape(t, -1), yb, yc.reshape(t, -1), xb, x, w["wgl"], w["wb"], w["wout"],
                     w["ln_g"], w["ln_b"], l)

    kv = _proj_plain(mem_b, w["wkv"], l, tn=D_MODEL, out_dtype=BF16, name="proj_kv")
    x, xb = _xattn(xb.reshape(batch, seq, d), x.reshape(batch, seq, d), kv.reshape(batch, MEM_LEN, 2 * d),
                   w["wq"], w["wo"], w["ln_g"], w["ln_b"], l)

    x, _ = _ffn(x.reshape(t, d), w["w12"], w["w3"], w["ln_g"], w["ln_b"], (l, 1))
    return x.reshape(batch, seq, d)


def kernel(x, mem, ln_g, ln_b, ffn_w12, ffn_w3, mix_w_in, dn_conv_w, dn_a_log, dn_dt_bias, dn_norm_w,
           lru_conv_w, lru_conv_b, lru_gate_w, lru_gate_b, lru_lambda, la_rel_bias, w_branch, mix_w_out,
           xa_wq, xa_wkv, xa_wo):
    w = _prepare(dict(
        ln_g=ln_g, ln_b=ln_b, ffn_w12=ffn_w12, ffn_w3=ffn_w3, mix_w_in=mix_w_in, dn_conv_w=dn_conv_w,
        dn_a_log=dn_a_log, dn_dt_bias=dn_dt_bias, dn_norm_w=dn_norm_w, lru_conv_w=lru_conv_w,
        lru_conv_b=lru_conv_b, lru_gate_w=lru_gate_w, lru_gate_b=lru_gate_b, lru_lambda=lru_lambda,
        la_rel_bias=la_rel_bias, w_branch=w_branch, mix_w_out=mix_w_out, xa_wq=xa_wq, xa_wkv=xa_wkv,
        xa_wo=xa_wo))
    batch = x.shape[0]
    mem_b = mem.reshape(batch * MEM_LEN, D_MODEL).astype(BF16)
    x = x.astype(F32)
    for l in range(DEPTH):
        x = _layer(x, mem_b, l, w)
    return x
```

```python
import functools

import numpy as np
import jax
import jax.numpy as jnp
from jax import lax
from jax.experimental import pallas as pl
from jax.experimental.pallas import tpu as pltpu

F32 = jnp.float32
BF16 = jnp.bfloat16

D_MODEL = 1024
DEPTH = 2
CHUNK = 64
CONV_W = 4
BR_W = 512
N_BRANCH = 3
DN_HEADS = 4
DN_DK = 128
DN_DV = 128
LRU_W = BR_W
LRU_BLOCKS = 8
LRU_BLK = LRU_W // LRU_BLOCKS
LRU_C = 8.0
LA_HEADS = 8
LA_HD = 64
LA_PAST = 8
REL_CLIP = 128
MEM_LEN = 256
XA_HEADS = 4
XA_HD = D_MODEL // XA_HEADS
D_FF = 2816
ALPHA = (2 * DEPTH) ** 0.25
LN_EPS = 1e-5
NORM_EPS = 1e-6
NEG_INF = -1e30

LANES = 128
SUBLANES = 8
VMEM_LIMIT = 56 * 1024 * 1024

_O_DZ = 3 * DN_HEADS * DN_DK
_O_DB = _O_DZ + DN_HEADS * DN_DV
_O_LX = _O_DB + 2 * DN_HEADS
_O_AQ = _O_LX + 2 * LRU_W
_O_GL = _O_AQ + 3 * LA_HEADS * LA_HD


def _params(*sem):
    return pltpu.CompilerParams(dimension_semantics=sem, vmem_limit_bytes=VMEM_LIMIT)


def _dot(a, b):
    return jnp.dot(a, b, preferred_element_type=F32)


def _dot_nt(a, b):
    return lax.dot_general(a, b, (((1,), (1,)), ((), ())), preferred_element_type=F32)


def _sigmoid(x):
    return 1.0 / (1.0 + jnp.exp(-x))


def _softplus(x):
    return jnp.maximum(x, 0.0) + jnp.log1p(jnp.exp(-jnp.abs(x)))


def _gelu_tanh(x):
    c = np.float32(np.sqrt(2.0 / np.pi))
    return 0.5 * x * (1.0 + jnp.tanh(c * (x + np.float32(0.044715) * (x * x * x))))


def _layer_norm(y, g, b):
    mu = jnp.mean(y, axis=-1, keepdims=True)
    d = y - mu
    var = jnp.mean(d * d, axis=-1, keepdims=True)
    return d * lax.rsqrt(var + LN_EPS) * g + b


def _pick(arr, *idx):
    shape = (None,) * len(idx) + tuple(arr.shape[len(idx):])
    index = tuple(idx) + (0,) * (arr.ndim - len(idx))
    return pl.BlockSpec(shape, lambda *_: index, pipeline_mode=pl.Buffered(1))


_FF_CHUNK = 256


def _ffn_kernel(x_ref, w12_ref, w3_ref, g_ref, b_ref, o_ref, ob_ref):
    x = x_ref[...]
    xb = x.astype(BF16)
    n = D_FF // _FF_CHUNK

    def gate_up(c):
        lo = c * _FF_CHUNK
        return (_dot(xb, w12_ref[:, lo:lo + _FF_CHUNK]),
                _dot(xb, w12_ref[:, D_FF + lo:D_FF + lo + _FF_CHUNK]))

    nxt = gate_up(0)
    acc = None
    for c in range(n):
        g, u = nxt
        if c + 1 < n:
            nxt = gate_up(c + 1)
        h = (g * _sigmoid(g) * u).astype(BF16)
        part = _dot(h, w3_ref[c * _FF_CHUNK:(c + 1) * _FF_CHUNK, :])
        acc = part if acc is None else acc + part
    out = _layer_norm(ALPHA * x + 0.5 * acc, g_ref[...], b_ref[...])
    o_ref[...] = out
    ob_ref[...] = out.astype(BF16)


def _ffn(x, w12, w3, ln_g, ln_b, sel, *, tm=1024):
    t, d = x.shape
    l, k = sel
    return pl.pallas_call(
        _ffn_kernel,
        grid=(t // tm,),
        in_specs=[
            pl.BlockSpec((tm, d), lambda i: (i, 0)),
            _pick(w12, l, k),
            _pick(w3, l, k),
            _pick(ln_g, l, 3 * k),
            _pick(ln_b, l, 3 * k),
        ],
        out_specs=[
            pl.BlockSpec((tm, d), lambda i: (i, 0)),
            pl.BlockSpec((tm, d), lambda i: (i, 0)),
        ],
        out_shape=[jax.ShapeDtypeStruct((t, d), F32), jax.ShapeDtypeStruct((t, d), BF16)],
        compiler_params=_params("parallel"),
        name="ffn_ln",
    )(x, w12, w3, ln_g, ln_b)


_HW = DN_HEADS * DN_DK
_C_QKV = 0
_C_Z = 3 * _HW
_C_BA = _C_Z + _HW
_C_LX = _C_BA + LANES
_C_LG = _C_LX + LRU_W
_C_ATT = _C_LG + LRU_W
_SEG = 256
_N_TAILS = (3 * _HW + LRU_W) // _SEG


def _causal_conv(y, tail, cw):
    row8 = lax.broadcasted_iota(jnp.int32, (SUBLANES, y.shape[1]), 0)
    acc = y * cw[CONV_W - 1:CONV_W, :]
    fix = jnp.zeros((SUBLANES, y.shape[1]), F32)
    for k in range(1, CONV_W):
        wk = cw[CONV_W - 1 - k:CONV_W - k, :]
        sh = pltpu.roll(y, k, axis=0)
        acc = acc + sh * wk
        prev = pltpu.roll(tail, k, axis=0)
        fix = fix + jnp.where(row8 < k, (prev - sh[:SUBLANES]) * wk, 0.0)
    return jnp.concatenate([acc[:SUBLANES] + fix, acc[SUBLANES:]], axis=0)


def _mixer_in_kernel(tiles_per_seq, x_ref, w_ref, dcw_ref, lcw_ref, lcb_ref,
                     qkvz_ref, ba_ref, xg_ref, att_ref, tail_ref):
    xb = x_ref[...]
    first = (pl.program_id(0) % tiles_per_seq) == 0
    rows = xb.shape[0]

    def proj(c0, width=_SEG):
        return _dot_nt(xb, w_ref[c0:c0 + width, :])

    def cols(j):
        return slice(j * _SEG, (j + 1) * _SEG)

    def conv(y, slot, cw):
        tail = jnp.where(first, 0.0, tail_ref[slot])
        tail_ref[slot] = y[rows - SUBLANES:]
        return _causal_conv(y, tail, cw)

    def delta_qkv(j, y):
        c = conv(y, j, dcw_ref[:, cols(j)])
        c = c * _sigmoid(c)
        if j * _SEG < 2 * _HW:
            scale = np.float32(DN_DK ** -0.5 if j * _SEG < _HW else 1.0)
            parts = []
            for h in range(_SEG // DN_DK):
                ch = c[:, h * DN_DK:(h + 1) * DN_DK]
                parts.append(ch * (lax.rsqrt(jnp.sum(ch * ch, axis=-1, keepdims=True) + NORM_EPS) * scale))
            c = jnp.concatenate(parts, axis=1)
        qkvz_ref[:, cols(j)] = c.astype(BF16)

    def z_part(j):
        qkvz_ref[:, 3 * _HW + j * _SEG:3 * _HW + (j + 1) * _SEG] = proj(_C_Z + j * _SEG).astype(BF16)

    def lru_x(j, y):
        slot = 3 * _HW // _SEG + j
        xg_ref[:, cols(j)] = (conv(y, slot, lcw_ref[:, cols(j)]) + lcb_ref[:, cols(j)]).astype(BF16)

    def lru_g(j, y):
        xg_ref[:, LRU_W + j * _SEG:LRU_W + (j + 1) * _SEG] = _gelu_tanh(y).astype(BF16)

    def att(j):
        att_ref[:, cols(j)] = proj(_C_ATT + j * _SEG).astype(BF16)

    heavy = ([(_C_QKV + j * _SEG, functools.partial(delta_qkv, j)) for j in range(3 * _HW // _SEG)]
             + [(_C_LX + j * _SEG, functools.partial(lru_x, j)) for j in range(LRU_W // _SEG)]
             + [(_C_LG + j * _SEG, functools.partial(lru_g, j)) for j in range(LRU_W // _SEG)])
    light = ([functools.partial(att, j) for j in range(3 * _HW // _SEG)]
             + [functools.partial(z_part, j) for j in range(_HW // _SEG)])
    for i, (c0, epilogue) in enumerate(heavy):
        epilogue(proj(c0))
        if i < len(light):
            light[i]()
    for g in light[len(heavy):]:
        g()
    ba_ref[...] = proj(_C_BA, LANES)


def _mixer_in(xb, w, dcw, lcw, lcb, l, *, seq, tm=512):
    t, d = xb.shape

    def tile(width):
        return pl.BlockSpec((tm, width), lambda i: (i, 0))

    widths = (4 * _HW, LANES, 2 * LRU_W, 3 * _HW)
    dtypes = (BF16, F32, BF16, BF16)
    return pl.pallas_call(
        functools.partial(_mixer_in_kernel, seq // tm),
        grid=(t // tm,),
        in_specs=[tile(d), _pick(w, l), _pick(dcw, l), _pick(lcw, l), _pick(lcb, l)],
        out_specs=[tile(wd) for wd in widths],
        out_shape=[jax.ShapeDtypeStruct((t, wd), dt) for wd, dt in zip(widths, dtypes)],
        scratch_shapes=[pltpu.VMEM((_N_TAILS, SUBLANES, _SEG), F32)],
        compiler_params=_params("arbitrary"),
        name="mixer_in",
    )(xb, w, dcw, lcw, lcb)


def _proj_plain_kernel(x_ref, w_ref, o_ref):
    o_ref[...] = _dot(x_ref[...], w_ref[...]).astype(o_ref.dtype)


def _proj_plain(xb, w, l, *, tn, out_dtype, name):
    t, d = xb.shape
    n = w.shape[2]
    return pl.pallas_call(
        _proj_plain_kernel,
        grid=(n // tn,),
        in_specs=[pl.BlockSpec((t, d), lambda j: (0, 0)), pl.BlockSpec((None, d, tn), lambda j: (l, 0, j))],
        out_specs=pl.BlockSpec((t, tn), lambda j: (0, j)),
        out_shape=jax.ShapeDtypeStruct((t, n), out_dtype),
        compiler_params=_params("parallel"),
        name=name,
    )(xb, w)


_PREP_CHUNKS = 8
_SCAN_CHUNKS = 4


def _segmented_cumsum_rows(x, seg):
    row = lax.broadcasted_iota(jnp.int32, x.shape, 0) & (seg - 1)
    d = 1
    while d < seg:
        x = x + jnp.where(row >= d, pltpu.roll(x, d, axis=0), 0.0)
        d *= 2
    return x


def _delta_prep_kernel(q_ref, k_ref, v_ref, ba_ref, alog_ref, dtb_ref,
                       wq_ref, u_ref, qk_ref, kdt_ref, gl_ref):
    row = lax.broadcasted_iota(jnp.int32, (CHUNK, CHUNK), 0)
    col = lax.broadcasted_iota(jnp.int32, (CHUNK, CHUNK), 1)
    lower = row >= col
    strict = row > col
    eye = jnp.where(row == col, 1.0, 0.0).astype(F32)
    zpad = jnp.zeros((LANES - CHUNK, LANES), F32)

    ba = ba_ref[0]
    beta_all = _sigmoid(ba)
    g_all = -jnp.exp(alog_ref[...]) * _softplus(ba + dtb_ref[...])
    g_cum = _segmented_cumsum_rows(g_all, CHUNK)
    g_cum_t = [jnp.transpose(g_cum[pr * LANES:(pr + 1) * LANES]) for pr in range(_PREP_CHUNKS // 2)]

    inst = [(ci, h) for ci in range(_PREP_CHUNKS) for h in range(DN_HEADS)]
    st = []
    for ci, h in inst:
        rs = slice(ci * CHUNK, (ci + 1) * CHUNK)
        hs = slice(h * DN_DK, (h + 1) * DN_DK)
        q = q_ref[0, rs, hs].astype(F32)
        k = k_ref[0, rs, hs].astype(F32)
        beta = beta_all[rs, h:h + 1]
        gc = g_cum[rs, DN_HEADS + h:DN_HEADS + h + 1]
        gr = g_cum_t[ci // 2][DN_HEADS + h:DN_HEADS + h + 1, (ci % 2) * CHUNK:(ci % 2 + 1) * CHUNK]
        decay = jnp.exp(jnp.where(lower, gc - gr, NEG_INF))
        kb = k * beta
        aq = _dot_nt(jnp.concatenate([kb, q], axis=0).astype(BF16), k.astype(BF16))
        st.append(dict(rs=rs, hs=hs, beta=beta, gc=gc, kb=kb, aq=aq, decay=decay))

    for s in st:
        s["p"] = -jnp.where(strict, s["aq"][:CHUNK] * s["decay"], 0.0)
        s["tinv"] = eye + s["p"]
    n = 1
    while 2 * n < CHUNK:
        for s in st:
            p16 = s["p"].astype(BF16)
            s["p"] = _dot(p16, p16)
        for s in st:
            s["tinv"] = s["tinv"] + _dot(s["tinv"].astype(BF16), s["p"].astype(BF16))
        n *= 2

    for s in st:
        exp_g = jnp.exp(s["gc"])
        v = v_ref[0, s["rs"], s["hs"]].astype(F32)
        rhs = jnp.concatenate([v * s["beta"], s["kb"] * exp_g], axis=1)
        s["sol"] = _dot(s["tinv"].astype(BF16), rhs.astype(BF16))
        s["exp_g"] = exp_g

    for (ci, h), s in zip(inst, st):
        q = q_ref[0, s["rs"], s["hs"]].astype(F32)
        k = k_ref[0, s["rs"], s["hs"]].astype(F32)
        g_last = s["gc"][CHUNK - 1:CHUNK, :]
        k_dec = k * jnp.exp(g_last - s["gc"])
        kdt = jnp.transpose(jnp.concatenate([k_dec, zpad], axis=0))[:, :CHUNK]
        wq_ref[0, ci, h * 2 * CHUNK:(h + 1) * 2 * CHUNK, :] = jnp.concatenate(
            [s["sol"][:, DN_DV:], q * s["exp_g"]], axis=0).astype(BF16)
        u_ref[0, ci, h * CHUNK:(h + 1) * CHUNK, :] = s["sol"][:, :DN_DV]
        qk_ref[0, ci, h * CHUNK:(h + 1) * CHUNK, :] = (s["aq"][CHUNK:] * s["decay"]).astype(BF16)
        kdt_ref[0, ci, h * DN_DK:(h + 1) * DN_DK, :] = kdt.astype(BF16)
        gl_ref[0, ci, h * SUBLANES:(h + 1) * SUBLANES, :] = jnp.broadcast_to(
            jnp.exp(g_last), (SUBLANES, LANES))


def _delta_prep(qkvz, ba, alog, dtb, l):
    batch, seq, _ = qkvz.shape
    n = seq // CHUNK
    rows = _PREP_CHUNKS * CHUNK
    hw = DN_HEADS * DN_DK

    def col_spec(cb):
        return pl.BlockSpec((1, rows, hw), lambda b, i: (b, i, cb))

    def out_spec(r, c):
        return pl.BlockSpec((1, _PREP_CHUNKS, r, c), lambda b, i: (b, i, 0, 0))

    return pl.pallas_call(
        _delta_prep_kernel,
        grid=(batch, n // _PREP_CHUNKS),
        in_specs=[col_spec(0), col_spec(1), col_spec(2),
                  pl.BlockSpec((1, rows, LANES), lambda b, i: (b, i, 0)),
                  _pick(alog, l), _pick(dtb, l)],
        out_specs=[out_spec(DN_HEADS * 2 * CHUNK, DN_DK), out_spec(DN_HEADS * CHUNK, DN_DV),
                   out_spec(DN_HEADS * CHUNK, CHUNK), out_spec(DN_HEADS * DN_DK, CHUNK),
                   out_spec(DN_HEADS * SUBLANES, LANES)],
        out_shape=[jax.ShapeDtypeStruct((batch, n, DN_HEADS * 2 * CHUNK, DN_DK), BF16),
                   jax.ShapeDtypeStruct((batch, n, DN_HEADS * CHUNK, DN_DV), F32),
                   jax.ShapeDtypeStruct((batch, n, DN_HEADS * CHUNK, CHUNK), BF16),
                   jax.ShapeDtypeStruct((batch, n, DN_HEADS * DN_DK, CHUNK), BF16),
                   jax.ShapeDtypeStruct((batch, n, DN_HEADS * SUBLANES, LANES), F32)],
        compiler_params=_params("parallel", "parallel"),
        name="delta_prep",
    )(qkvz, qkvz, qkvz, ba, alog, dtb)


def _delta_scan_kernel(batch, wq_ref, u_ref, qk_ref, kdt_ref, gl_ref, z_ref, nw_ref, o_ref, state_ref):
    c = pl.program_id(0)

    @pl.when(c == 0)
    def _():
        state_ref[...] = jnp.zeros_like(state_ref)

    chains = [(b, h) for b in range(batch) for h in range(DN_HEADS)]
    for cc in range(_SCAN_CHUNKS):
        rs = slice(cc * CHUNK, (cc + 1) * CHUNK)
        r = [_dot(wq_ref[b, cc, h * 2 * CHUNK:(h + 1) * 2 * CHUNK, :],
                  state_ref[b * DN_HEADS + h].astype(BF16)) for b, h in chains]
        v_new = [(u_ref[b, cc, h * CHUNK:(h + 1) * CHUNK, :] - r[i][:CHUNK]).astype(BF16)
                 for i, (b, h) in enumerate(chains)]
        kv = [_dot(kdt_ref[b, cc, h * DN_DK:(h + 1) * DN_DK, :], v_new[i]) for i, (b, h) in enumerate(chains)]
        qv = [_dot(qk_ref[b, cc, h * CHUNK:(h + 1) * CHUNK, :], v_new[i]) for i, (b, h) in enumerate(chains)]
        for i, (b, h) in enumerate(chains):
            state_ref[b * DN_HEADS + h] = (
                state_ref[b * DN_HEADS + h] * gl_ref[b, cc, h * SUBLANES:h * SUBLANES + 1, :] + kv[i])
            o = r[i][CHUNK:] + qv[i]
            z = z_ref[b, rs, h * DN_DV:(h + 1) * DN_DV].astype(F32)
            o = o * lax.rsqrt(jnp.mean(o * o, axis=-1, keepdims=True) + NORM_EPS)
            o = o * nw_ref[...] * (z * _sigmoid(z))
            o_ref[b, rs, h * DN_DV:(h + 1) * DN_DV] = o.astype(o_ref.dtype)


def _delta_scan(wq, u, qk, kdt, gl, qkvz, nw, l):
    batch, seq, _ = qkvz.shape
    hw = DN_HEADS * DN_DV

    rows = _SCAN_CHUNKS * CHUNK

    def step_spec(a):
        return pl.BlockSpec((batch, _SCAN_CHUNKS) + a.shape[2:], lambda c: (0, c, 0, 0))

    return pl.pallas_call(
        functools.partial(_delta_scan_kernel, batch),
        grid=(seq // rows,),
        in_specs=[step_spec(wq), step_spec(u), step_spec(qk), step_spec(kdt), step_spec(gl),
                  pl.BlockSpec((batch, rows, hw), lambda c: (0, c, 3)),
                  _pick(nw, l)],
        out_specs=pl.BlockSpec((batch, rows, hw), lambda c: (0, c, 0)),
        out_shape=jax.ShapeDtypeStruct((batch, seq, hw), BF16),
        scratch_shapes=[pltpu.VMEM((batch * DN_HEADS, DN_DK, DN_DV), F32)],
        compiler_params=_params("arbitrary"),
        name="delta_scan",
    )(wq, u, qk, kdt, gl, qkvz, nw)


def _lru_kernel(tiles_per_seq, xg_ref, wg_ref, gb_ref, lam_ref, o_ref, h_ref):
    i = pl.program_id(0)
    rows = xg_ref.shape[0]
    xc = xg_ref[:, :LRU_W].astype(F32)
    gate = xg_ref[:, LRU_W:].astype(F32)
    gates = _dot(xg_ref[:, :LRU_W], wg_ref[...]) + gb_ref[...]
    r = _sigmoid(gates[:, :LRU_W])
    ig = _sigmoid(gates[:, LRU_W:])
    log_a = -LRU_C * r * _softplus(-lam_ref[...])
    a = jnp.exp(log_a)
    u = xc * ig * jnp.sqrt(-jnp.tanh(log_a) * (1.0 + a * a))
    row = lax.broadcasted_iota(jnp.int32, a.shape, 0)
    d = 1
    while d < rows:
        a_sh = jnp.where(row >= d, pltpu.roll(a, d, axis=0), 1.0)
        u_sh = jnp.where(row >= d, pltpu.roll(u, d, axis=0), 0.0)
        u = a * u_sh + u
        a = a * a_sh
        d *= 2
    first = (i % tiles_per_seq) == 0
    h_prev = jnp.where(first, 0.0, h_ref[...])
    h = u + a * h_prev
    h_ref[...] = h[rows - 1:rows, :]
    o_ref[...] = (h * gate).astype(o_ref.dtype)


def _lru(xg, wg, gb, lam, l, *, seq, tm=256):
    t = xg.shape[0]
    return pl.pallas_call(
        functools.partial(_lru_kernel, seq // tm),
        grid=(t // tm,),
        in_specs=[
            pl.BlockSpec((tm, 2 * LRU_W), lambda i: (i, 0)),
            _pick(wg, l), _pick(gb, l), _pick(lam, l),
        ],
        out_specs=pl.BlockSpec((tm, LRU_W), lambda i: (i, 0)),
        out_shape=jax.ShapeDtypeStruct((t, LRU_W), BF16),
        scratch_shapes=[pltpu.VMEM((1, LRU_W), F32)],
        compiler_params=_params("arbitrary"),
        name="rg_lru",
    )(xg, wg, gb, lam)


_QB = 4 * CHUNK
_KB = 3
_CHUNK_SHIFT = CHUNK.bit_length() - 1


def _band_kernel(q_ref, k0_ref, k1_ref, k2_ref, v0_ref, v1_ref, v2_ref, rel_ref, o_ref, bias_ref):
    i = pl.program_id(1)

    @pl.when((pl.program_id(0) == 0) & (i == 0))
    def _():
        qpos = lax.broadcasted_iota(jnp.int32, (_QB, _QB), 0)
        kcol = lax.broadcasted_iota(jnp.int32, (_QB, _QB), 1)
        for m in range(_KB):
            chunk_off = ((kcol + (m - (_KB - 1)) * _QB) >> _CHUNK_SHIFT) - (qpos >> _CHUNK_SHIFT)
            valid = (chunk_off <= 0) & (chunk_off >= -LA_PAST)
            for h in range(LA_HEADS):
                ev = jnp.broadcast_to(rel_ref[h, m:m + 1, :], (_QB, 2 * _QB))
                toeplitz = pltpu.roll(ev, _QB, axis=1, stride=1, stride_axis=0)[:, :_QB]
                bias_ref[h, m] = jnp.where(valid, toeplitz, NEG_INF)

    k_refs = (k0_ref, k1_ref, k2_ref)
    v_refs = (v0_ref, v1_ref, v2_ref)
    lane = lax.broadcasted_iota(jnp.int32, (1, LANES), 1)
    half_sel = (lane < LA_HD, lane >= LA_HD)

    def attend(ms):
        def scores(h):
            ps = slice((h // 2) * LANES, (h // 2 + 1) * LANES)
            qh = jnp.where(half_sel[h % 2], q_ref[0, :, ps], jnp.zeros((), BF16))
            return [_dot_nt(qh, k_refs[m][0, :, ps]) for m in ms]

        s_next = scores(0)
        pair_out = None
        for h in range(LA_HEADS):
            ps = slice((h // 2) * LANES, (h // 2 + 1) * LANES)
            sel = half_sel[h % 2]
            s = s_next
            if h + 1 < LA_HEADS:
                s_next = scores(h + 1)
            s = [sm + bias_ref[h, m] for sm, m in zip(s, ms)]
            mx = s[0].max(axis=-1, keepdims=True)
            for sm in s[1:]:
                mx = jnp.maximum(mx, sm.max(axis=-1, keepdims=True))
            acc = None
            for sm, m in zip(s, ms):
                vh = jnp.where(sel, v_refs[m][0, :, ps], jnp.ones((), BF16))
                part = _dot(jnp.exp((sm - mx).astype(BF16)), vh)
                acc = part if acc is None else acc + part
            den = pltpu.roll(acc, LA_HD, axis=1)
            out = jnp.where(sel, acc / den, 0.0)
            if h % 2 == 0:
                pair_out = out
            else:
                o_ref[0, :, ps] = (pair_out + out).astype(o_ref.dtype)

    for nvalid in range(1, _KB + 1):
        pl.when(jnp.minimum(i, _KB - 1) == nvalid - 1)(
            functools.partial(attend, tuple(range(_KB - nvalid, _KB))))


def _band_attention(qkv, rel_rows, l, *, batch, seq):
    hw = LA_HEADS * LA_HD
    nblk = seq // _QB

    def kv_spec(col, m):
        return pl.BlockSpec((1, _QB, hw), lambda b, i: (b, jnp.maximum(i - (_KB - 1 - m), 0), col))

    return pl.pallas_call(
        _band_kernel,
        grid=(batch, nblk),
        in_specs=[pl.BlockSpec((1, _QB, hw), lambda b, i: (b, i, 0))]
        + [kv_spec(1, m) for m in range(_KB)]
        + [kv_spec(2, m) for m in range(_KB)]
        + [_pick(rel_rows, l)],
        out_specs=pl.BlockSpec((1, _QB, hw), lambda b, i: (b, i, 0)),
        out_shape=jax.ShapeDtypeStruct((batch, seq, hw), BF16),
        scratch_shapes=[pltpu.VMEM((LA_HEADS, _KB, _QB, _QB), F32)],
        compiler_params=_params("arbitrary", "arbitrary"),
        name="band_attention",
    )(qkv, qkv, qkv, qkv, qkv, qkv, qkv, rel_rows)


def _band_rel_rows(rel_table):
    t = rel_table.astype(F32)
    lo = _KB * _QB - REL_CLIP
    hi = _QB - 1 - REL_CLIP
    full = jnp.concatenate([jnp.broadcast_to(t[..., :1], t.shape[:-1] + (lo,)), t,
                            jnp.broadcast_to(t[..., -1:], t.shape[:-1] + (hi,))], axis=-1)
    return jnp.stack([full[..., m * _QB:(m + 2) * _QB] for m in range(_KB)], axis=-2)


_SUB_TILES = 2


def _sub_tiles(rows):
    step = rows // _SUB_TILES
    return [slice(s * step, (s + 1) * step) for s in range(_SUB_TILES)]


def _mix_out_kernel(ya_ref, yb_ref, yc_ref, xb_ref, x_ref, wgl_ref, wb_ref, wo_ref, g_ref, b_ref,
                    o_ref, ob_ref):
    ys = (ya_ref, yb_ref, yc_ref)
    subs = _sub_tiles(x_ref.shape[0])
    ups = [[_dot(ys[r][rs, :], wb_ref[r]) for r in range(N_BRANCH)] for rs in subs]
    gl = [[_dot_nt(xb_ref[rs, :], wgl_ref[r * D_MODEL:(r + 1) * D_MODEL, :]) for r in range(N_BRANCH)]
          for rs in subs]
    merged = []
    for up, g in zip(ups, gl):
        m = None
        for r in range(N_BRANCH):
            term = _sigmoid(g[r]) * up[r]
            m = term if m is None else m + term
        merged.append(m.astype(BF16))
    y = [_dot(m, wo_ref[...]) for m in merged]
    for rs, yy in zip(subs, y):
        out = _layer_norm(ALPHA * x_ref[rs, :] + yy, g_ref[...], b_ref[...])
        o_ref[rs, :] = out
        ob_ref[rs, :] = out.astype(BF16)


def _mix_out(ya, yb, yc, xb, x, wgl, wb, wo, ln_g, ln_b, l, *, tm=512):
    t, d = x.shape
    return pl.pallas_call(
        _mix_out_kernel,
        grid=(t // tm,),
        in_specs=[
            pl.BlockSpec((tm, BR_W), lambda i: (i, 0)),
            pl.BlockSpec((tm, BR_W), lambda i: (i, 0)),
            pl.BlockSpec((tm, BR_W), lambda i: (i, 0)),
            pl.BlockSpec((tm, d), lambda i: (i, 0)),
            pl.BlockSpec((tm, d), lambda i: (i, 0)),
            _pick(wgl, l), _pick(wb, l), _pick(wo, l), _pick(ln_g, l, 1), _pick(ln_b, l, 1),
        ],
        out_specs=[pl.BlockSpec((tm, d), lambda i: (i, 0)), pl.BlockSpec((tm, d), lambda i: (i, 0))],
        out_shape=[jax.ShapeDtypeStruct((t, d), F32), jax.ShapeDtypeStruct((t, d), BF16)],
        compiler_params=_params("parallel"),
        name="mix_out",
    )(ya, yb, yc, xb, x, wgl, wb, wo, ln_g, ln_b)


def _xattn_kernel(xb_ref, x_ref, kv_ref, wq_ref, wo_ref, g_ref, b_ref, o_ref, ob_ref):
    subs = _sub_tiles(x_ref.shape[1])
    q = [(_dot(xb_ref[0, rs, :], wq_ref[...]) * np.float32(XA_HD ** -0.5)).astype(BF16) for rs in subs]

    def scores(task):
        s, h = task
        sl = slice(h * XA_HD, (h + 1) * XA_HD)
        return _dot_nt(q[s][:, sl], kv_ref[0, :, sl])

    tasks = [(s, h) for h in range(XA_HEADS) for s in range(len(subs))]
    outs = [[None] * XA_HEADS for _ in subs]
    s_next = scores(tasks[0])
    for ti, (s, h) in enumerate(tasks):
        v = kv_ref[0, :, D_MODEL + h * XA_HD:D_MODEL + (h + 1) * XA_HD]
        sc = s_next
        if ti + 1 < len(tasks):
            s_next = scores(tasks[ti + 1])
        p = jnp.exp(sc - sc.max(axis=-1, keepdims=True))
        den = p.sum(axis=-1, keepdims=True)
        outs[s][h] = (_dot(p.astype(BF16), v) / den).astype(BF16)
    y = [_dot(jnp.concatenate(o, axis=1), wo_ref[...]) for o in outs]
    for rs, yy in zip(subs, y):
        out = _layer_norm(ALPHA * x_ref[0, rs, :] + yy, g_ref[...], b_ref[...])
        o_ref[0, rs, :] = out
        ob_ref[0, rs, :] = out.astype(BF16)


def _xattn(xb, x, kv, wq, wo, ln_g, ln_b, l, *, tm=512):
    batch, seq, d = x.shape
    return pl.pallas_call(
        _xattn_kernel,
        grid=(batch, seq // tm),
        in_specs=[
            pl.BlockSpec((1, tm, d), lambda bi, i: (bi, i, 0)),
            pl.BlockSpec((1, tm, d), lambda bi, i: (bi, i, 0)),
            pl.BlockSpec((1, MEM_LEN, 2 * d), lambda bi, i: (bi, 0, 0)),
            _pick(wq, l), _pick(wo, l), _pick(ln_g, l, 2), _pick(ln_b, l, 2),
        ],
        out_specs=[
            pl.BlockSpec((1, tm, d), lambda bi, i: (bi, i, 0)),
            pl.BlockSpec((1, tm, d), lambda bi, i: (bi, i, 0)),
        ],
        out_shape=[jax.ShapeDtypeStruct((batch, seq, d), F32), jax.ShapeDtypeStruct((batch, seq, d), BF16)],
        compiler_params=_params("parallel", "parallel"),
        name="mem_xattn",
    )(xb, x, kv, wq, wo, ln_g, ln_b)


def _prepare(p):
    d = D_MODEL
    depth = p["ln_g"].shape[0]
    wt = jnp.swapaxes(p["mix_w_in"].astype(F32), 1, 2)
    hw = LA_HEADS * LA_HD
    w_packed = jnp.concatenate([
        wt[:, :_O_DB],
        jnp.pad(wt[:, _O_DB:_O_LX], ((0, 0), (0, LANES - 2 * DN_HEADS), (0, 0))),
        wt[:, _O_LX:_O_AQ],
        wt[:, _O_AQ:_O_AQ + hw] * np.float32(LA_HD ** -0.5),
        wt[:, _O_AQ + hw:_O_GL]], axis=1).astype(BF16)
    wgl = wt[:, _O_GL:].astype(BF16)

    def lane_row(v):
        return jnp.pad(v.astype(F32), ((0, 0), (DN_HEADS, LANES - 2 * DN_HEADS)))[:, None, :]

    eye = jnp.eye(LRU_BLOCKS, dtype=F32)
    wg = jnp.einsum("lgnde,nm->lgndme", p["lru_gate_w"].astype(F32), eye)
    wg = wg.reshape(depth, 2, LRU_W, LRU_W)
    wg = jnp.concatenate([wg[:, 0], wg[:, 1]], axis=2).astype(BF16)
    return dict(
        ln_g=p["ln_g"].astype(F32).reshape(depth, 4, 1, d),
        ln_b=p["ln_b"].astype(F32).reshape(depth, 4, 1, d),
        w12=p["ffn_w12"].astype(BF16), w3=p["ffn_w3"].astype(BF16),
        w_in=w_packed, wgl=wgl,
        dcw=p["dn_conv_w"].astype(F32), lcw=p["lru_conv_w"].astype(F32),
        lcb=p["lru_conv_b"].astype(F32)[:, None, :],
        alog=lane_row(p["dn_a_log"]), dtb=lane_row(p["dn_dt_bias"]),
        nw=p["dn_norm_w"].astype(F32)[:, None, :],
        wg=wg, gb=p["lru_gate_b"].astype(F32).reshape(depth, 1, 2 * LRU_W),
        lam=p["lru_lambda"].astype(F32)[:, None, :],
        rel=_band_rel_rows(p["la_rel_bias"]),
        wb=p["w_branch"].astype(BF16), wout=p["mix_w_out"].astype(BF16),
        wq=p["xa_wq"].astype(BF16), wkv=p["xa_wkv"].astype(BF16), wo=p["xa_wo"].astype(BF16))


def _layer(x, mem_b, l, w):
    batch, seq, d = x.shape
    t = batch * seq

    x, xb = _ffn(x.reshape(t, d), w["w12"], w["w3"], w["ln_g"], w["ln_b"], (l, 0))

    qkvz, ba, xg, aqkv = _mixer_in(xb, w["w_in"], w["dcw"], w["lcw"], w["lcb"], l, seq=seq)
    qkvz = qkvz.reshape(batch, seq, -1)
    factors = _delta_prep(qkvz, ba.reshape(batch, seq, LANES), w["alog"], w["dtb"], l)
    ya = _delta_scan(*factors, qkvz, w["nw"], l)
    yb = _lru(xg, w["wg"], w["gb"], w["lam"], l, seq=seq)
    yc = _band_attention(aqkv.reshape(batch, seq, -1), w["rel"], l, batch=batch, seq=seq)
    x, xb = _mix_out(ya.reshape(t, -1), yb, yc.reshape(t, -1), xb, x, w["wgl"], w["wb"], w["wout"],
                     w["ln_g"], w["ln_b"], l)

    kv = _proj_plain(mem_b, w["wkv"], l, tn=D_MODEL, out_dtype=BF16, name="proj_kv")
    x, xb = _xattn(xb.reshape(batch, seq, d), x.reshape(batch, seq, d), kv.reshape(batch, MEM_LEN, 2 * d),
                   w["wq"], w["wo"], w["ln_g"], w["ln_b"], l)

    x, _ = _ffn(x.reshape(t, d), w["w12"], w["w3"], w["ln_g"], w["ln_b"], (l, 1))
    return x.reshape(batch, seq, d)


def kernel(x, mem, ln_g, ln_b, ffn_w12, ffn_w3, mix_w_in, dn_conv_w, dn_a_log, dn_dt_bias, dn_norm_w,
           lru_conv_w, lru_conv_b, lru_gate_w, lru_gate_b, lru_lambda, la_rel_bias, w_branch, mix_w_out,
           xa_wq, xa_wkv, xa_wo):
    w = _prepare(dict(
        ln_g=ln_g, ln_b=ln_b, ffn_w12=ffn_w12, ffn_w3=ffn_w3, mix_w_in=mix_w_in, dn_conv_w=dn_conv_w,
        dn_a_log=dn_a_log, dn_dt_bias=dn_dt_bias, dn_norm_w=dn_norm_w, lru_conv_w=lru_conv_w,
        lru_conv_b=lru_conv_b, lru_gate_w=lru_gate_w, lru_gate_b=lru_gate_b, lru_lambda=lru_lambda,
        la_rel_bias=la_rel_bias, w_branch=w_branch, mix_w_out=mix_w_out, xa_wq=xa_wq, xa_wkv=xa_wkv,
        xa_wo=xa_wo))
    batch = x.shape[0]
    mem_b = mem.reshape(batch * MEM_LEN, D_MODEL).astype(BF16)
    x = x.astype(F32)
    for l in range(DEPTH):
        x = _layer(x, mem_b, l, w)
    return x
```

```python
import functools

import numpy as np
import jax
import jax.numpy as jnp
from jax import lax
from jax.experimental import pallas as pl
from jax.experimental.pallas import tpu as pltpu

F32 = jnp.float32
BF16 = jnp.bfloat16

D_MODEL = 1024
DEPTH = 2
CHUNK = 64
CONV_W = 4
BR_W = 512
N_BRANCH = 3
DN_HEADS = 4
DN_DK = 128
DN_DV = 128
LRU_W = BR_W
LRU_BLOCKS = 8
LRU_BLK = LRU_W // LRU_BLOCKS
LRU_C = 8.0
LA_HEADS = 8
LA_HD = 64
LA_PAST = 8
REL_CLIP = 128
MEM_LEN = 256
XA_HEADS = 4
XA_HD = D_MODEL // XA_HEADS
D_FF = 2816
ALPHA = (2 * DEPTH) ** 0.25
LN_EPS = 1e-5
NORM_EPS = 1e-6
NEG_INF = -1e30

LANES = 128
SUBLANES = 8
VMEM_LIMIT = 56 * 1024 * 1024

_O_DZ = 3 * DN_HEADS * DN_DK
_O_DB = _O_DZ + DN_HEADS * DN_DV
_O_LX = _O_DB + 2 * DN_HEADS
_O_AQ = _O_LX + 2 * LRU_W
_O_GL = _O_AQ + 3 * LA_HEADS * LA_HD


def _params(*sem):
    return pltpu.CompilerParams(dimension_semantics=sem, vmem_limit_bytes=VMEM_LIMIT)


def _dot(a, b):
    return jnp.dot(a, b, preferred_element_type=F32)


def _dot_nt(a, b):
    return lax.dot_general(a, b, (((1,), (1,)), ((), ())), preferred_element_type=F32)


def _sigmoid(x):
    return 1.0 / (1.0 + jnp.exp(-x))


def _softplus(x):
    return jnp.maximum(x, 0.0) + jnp.log1p(jnp.exp(-jnp.abs(x)))


def _gelu_tanh(x):
    c = np.float32(np.sqrt(2.0 / np.pi))
    return 0.5 * x * (1.0 + jnp.tanh(c * (x + np.float32(0.044715) * (x * x * x))))


def _layer_norm(y, g, b):
    mu = jnp.mean(y, axis=-1, keepdims=True)
    d = y - mu
    var = jnp.mean(d * d, axis=-1, keepdims=True)
    return d * lax.rsqrt(var + LN_EPS) * g + b


def _pick(arr, *idx):
    shape = (None,) * len(idx) + tuple(arr.shape[len(idx):])
    index = tuple(idx) + (0,) * (arr.ndim - len(idx))
    return pl.BlockSpec(shape, lambda *_: index, pipeline_mode=pl.Buffered(1))


_FF_CHUNK = 256


def _ffn_kernel(x_ref, w12_ref, w3_ref, g_ref, b_ref, o_ref, ob_ref):
    x = x_ref[...]
    xb = x.astype(BF16)
    n = D_FF // _FF_CHUNK

    def gate_up(c):
        lo = c * _FF_CHUNK
        return (_dot(xb, w12_ref[:, lo:lo + _FF_CHUNK]),
                _dot(xb, w12_ref[:, D_FF + lo:D_FF + lo + _FF_CHUNK]))

    nxt = gate_up(0)
    acc = None
    for c in range(n):
        g, u = nxt
        if c + 1 < n:
            nxt = gate_up(c + 1)
        h = (g * _sigmoid(g) * u).astype(BF16)
        part = _dot(h, w3_ref[c * _FF_CHUNK:(c + 1) * _FF_CHUNK, :])
        acc = part if acc is None else acc + part
    out = _layer_norm(ALPHA * x + 0.5 * acc, g_ref[...], b_ref[...])
    o_ref[...] = out
    ob_ref[...] = out.astype(BF16)


def _ffn(x, w12, w3, ln_g, ln_b, sel, *, tm=1024):
    t, d = x.shape
    l, k = sel
    return pl.pallas_call(
        _ffn_kernel,
        grid=(t // tm,),
        in_specs=[
            pl.BlockSpec((tm, d), lambda i: (i, 0)),
            _pick(w12, l, k),
            _pick(w3, l, k),
            _pick(ln_g, l, 3 * k),
            _pick(ln_b, l, 3 * k),
        ],
        out_specs=[
            pl.BlockSpec((tm, d), lambda i: (i, 0)),
            pl.BlockSpec((tm, d), lambda i: (i, 0)),
        ],
        out_shape=[jax.ShapeDtypeStruct((t, d), F32), jax.ShapeDtypeStruct((t, d), BF16)],
        compiler_params=_params("parallel"),
        name="ffn_ln",
    )(x, w12, w3, ln_g, ln_b)


_HW = DN_HEADS * DN_DK
_C_QKV = 0
_C_Z = 3 * _HW
_C_BA = _C_Z + _HW
_C_LX = _C_BA + LANES
_C_LG = _C_LX + LRU_W
_C_ATT = _C_LG + LRU_W
_SEG = 256
_N_TAILS = (3 * _HW + LRU_W) // _SEG


def _causal_conv(y, tail, cw):
    row8 = lax.broadcasted_iota(jnp.int32, (SUBLANES, y.shape[1]), 0)
    acc = y * cw[CONV_W - 1:CONV_W, :]
    fix = jnp.zeros((SUBLANES, y.shape[1]), F32)
    for k in range(1, CONV_W):
        wk = cw[CONV_W - 1 - k:CONV_W - k, :]
        sh = pltpu.roll(y, k, axis=0)
        acc = acc + sh * wk
        prev = pltpu.roll(tail, k, axis=0)
        fix = fix + jnp.where(row8 < k, (prev - sh[:SUBLANES]) * wk, 0.0)
    return jnp.concatenate([acc[:SUBLANES] + fix, acc[SUBLANES:]], axis=0)


def _mixer_in_kernel(tiles_per_seq, x_ref, w_ref, dcw_ref, lcw_ref, lcb_ref,
                     qkvz_ref, ba_ref, xg_ref, att_ref, tail_ref):
    xb = x_ref[...]
    first = (pl.program_id(0) % tiles_per_seq) == 0
    rows = xb.shape[0]

    def proj(c0, width=_SEG):
        return _dot(xb, w_ref[:, c0:c0 + width])

    def cols(j):
        return slice(j * _SEG, (j + 1) * _SEG)

    def conv(y, slot, cw):
        tail = jnp.where(first, 0.0, tail_ref[slot])
        tail_ref[slot] = y[rows - SUBLANES:]
        return _causal_conv(y, tail, cw)

    def delta_qkv(j, y):
        c = conv(y, j, dcw_ref[:, cols(j)])
        c = c * _sigmoid(c)
        if j * _SEG < 2 * _HW:
            scale = np.float32(DN_DK ** -0.5 if j * _SEG < _HW else 1.0)
            parts = []
            for h in range(_SEG // DN_DK):
                ch = c[:, h * DN_DK:(h + 1) * DN_DK]
                parts.append(ch * (lax.rsqrt(jnp.sum(ch * ch, axis=-1, keepdims=True) + NORM_EPS) * scale))
            c = jnp.concatenate(parts, axis=1)
        qkvz_ref[:, cols(j)] = c.astype(BF16)

    def z_part(j):
        qkvz_ref[:, 3 * _HW + j * _SEG:3 * _HW + (j + 1) * _SEG] = proj(_C_Z + j * _SEG).astype(BF16)

    def lru_x(j, y):
        slot = 3 * _HW // _SEG + j
        xg_ref[:, cols(j)] = (conv(y, slot, lcw_ref[:, cols(j)]) + lcb_ref[:, cols(j)]).astype(BF16)

    def lru_g(j, y):
        xg_ref[:, LRU_W + j * _SEG:LRU_W + (j + 1) * _SEG] = _gelu_tanh(y).astype(BF16)

    def att(j):
        att_ref[:, cols(j)] = proj(_C_ATT + j * _SEG).astype(BF16)

    heavy = ([(_C_QKV + j * _SEG, functools.partial(delta_qkv, j)) for j in range(3 * _HW // _SEG)]
             + [(_C_LX + j * _SEG, functools.partial(lru_x, j)) for j in range(LRU_W // _SEG)]
             + [(_C_LG + j * _SEG, functools.partial(lru_g, j)) for j in range(LRU_W // _SEG)])
    light = ([functools.partial(att, j) for j in range(3 * _HW // _SEG)]
             + [functools.partial(z_part, j) for j in range(_HW // _SEG)])
    for i, (c0, epilogue) in enumerate(heavy):
        epilogue(proj(c0))
        if i < len(light):
            light[i]()
    for g in light[len(heavy):]:
        g()
    ba_ref[...] = proj(_C_BA, LANES)


def _mixer_in(xb, w, dcw, lcw, lcb, l, *, seq, tm=512):
    t, d = xb.shape

    def tile(width):
        return pl.BlockSpec((tm, width), lambda i: (i, 0))

    widths = (4 * _HW, LANES, 2 * LRU_W, 3 * _HW)
    dtypes = (BF16, F32, BF16, BF16)
    return pl.pallas_call(
        functools.partial(_mixer_in_kernel, seq // tm),
        grid=(t // tm,),
        in_specs=[tile(d), _pick(w, l), _pick(dcw, l), _pick(lcw, l), _pick(lcb, l)],
        out_specs=[tile(wd) for wd in widths],
        out_shape=[jax.ShapeDtypeStruct((t, wd), dt) for wd, dt in zip(widths, dtypes)],
        scratch_shapes=[pltpu.VMEM((_N_TAILS, SUBLANES, _SEG), F32)],
        compiler_params=_params("arbitrary"),
        name="mixer_in",
    )(xb, w, dcw, lcw, lcb)


def _proj_plain_kernel(x_ref, w_ref, o_ref):
    o_ref[...] = _dot(x_ref[...], w_ref[...]).astype(o_ref.dtype)


def _proj_plain(xb, w, l, *, tn, out_dtype, name):
    t, d = xb.shape
    n = w.shape[2]
    return pl.pallas_call(
        _proj_plain_kernel,
        grid=(n // tn,),
        in_specs=[pl.BlockSpec((t, d), lambda j: (0, 0)), pl.BlockSpec((None, d, tn), lambda j: (l, 0, j))],
        out_specs=pl.BlockSpec((t, tn), lambda j: (0, j)),
        out_shape=jax.ShapeDtypeStruct((t, n), out_dtype),
        compiler_params=_params("parallel"),
        name=name,
    )(xb, w)


_PREP_CHUNKS = 8
_SCAN_CHUNKS = 4


def _segmented_cumsum_rows(x, seg):
    row = lax.broadcasted_iota(jnp.int32, x.shape, 0) & (seg - 1)
    d = 1
    while d < seg:
        x = x + jnp.where(row >= d, pltpu.roll(x, d, axis=0), 0.0)
        d *= 2
    return x


def _delta_prep_kernel(q_ref, k_ref, v_ref, ba_ref, alog_ref, dtb_ref,
                       wq_ref, u_ref, qk_ref, kdt_ref, gl_ref):
    row = lax.broadcasted_iota(jnp.int32, (CHUNK, CHUNK), 0)
    col = lax.broadcasted_iota(jnp.int32, (CHUNK, CHUNK), 1)
    lower = row >= col
    strict = row > col
    eye = jnp.where(row == col, 1.0, 0.0).astype(F32)
    zpad = jnp.zeros((LANES - CHUNK, LANES), F32)

    ba = ba_ref[0]
    beta_all = _sigmoid(ba)
    g_all = -jnp.exp(alog_ref[...]) * _softplus(ba + dtb_ref[...])
    g_cum = _segmented_cumsum_rows(g_all, CHUNK)
    g_cum_t = [jnp.transpose(g_cum[pr * LANES:(pr + 1) * LANES]) for pr in range(_PREP_CHUNKS // 2)]

    inst = [(ci, h) for ci in range(_PREP_CHUNKS) for h in range(DN_HEADS)]
    st = []
    for ci, h in inst:
        rs = slice(ci * CHUNK, (ci + 1) * CHUNK)
        hs = slice(h * DN_DK, (h + 1) * DN_DK)
        q = q_ref[0, rs, hs].astype(F32)
        k = k_ref[0, rs, hs].astype(F32)
        beta = beta_all[rs, h:h + 1]
        gc = g_cum[rs, DN_HEADS + h:DN_HEADS + h + 1]
        gr = g_cum_t[ci // 2][DN_HEADS + h:DN_HEADS + h + 1, (ci % 2) * CHUNK:(ci % 2 + 1) * CHUNK]
        decay = jnp.exp(jnp.where(lower, gc - gr, NEG_INF))
        kb = k * beta
        aq = _dot_nt(jnp.concatenate([kb, q], axis=0).astype(BF16), k.astype(BF16))
        st.append(dict(rs=rs, hs=hs, beta=beta, gc=gc, kb=kb, aq=aq, decay=decay))

    for s in st:
        s["p"] = -jnp.where(strict, s["aq"][:CHUNK] * s["decay"], 0.0)
        s["tinv"] = eye + s["p"]
    n = 1
    while 2 * n < CHUNK:
        for s in st:
            p16 = s["p"].astype(BF16)
            s["p"] = _dot(p16, p16)
        for s in st:
            s["tinv"] = s["tinv"] + _dot(s["tinv"].astype(BF16), s["p"].astype(BF16))
        n *= 2

    for s in st:
        exp_g = jnp.exp(s["gc"])
        v = v_ref[0, s["rs"], s["hs"]].astype(F32)
        rhs = jnp.concatenate([v * s["beta"], s["kb"] * exp_g], axis=1)
        s["sol"] = _dot(s["tinv"].astype(BF16), rhs.astype(BF16))
        s["exp_g"] = exp_g

    for (ci, h), s in zip(inst, st):
        q = q_ref[0, s["rs"], s["hs"]].astype(F32)
        k = k_ref[0, s["rs"], s["hs"]].astype(F32)
        g_last = s["gc"][CHUNK - 1:CHUNK, :]
        k_dec = k * jnp.exp(g_last - s["gc"])
        kdt = jnp.transpose(jnp.concatenate([k_dec, zpad], axis=0))[:, :CHUNK]
        wq_ref[0, ci, h * 2 * CHUNK:(h + 1) * 2 * CHUNK, :] = jnp.concatenate(
            [s["sol"][:, DN_DV:], q * s["exp_g"]], axis=0).astype(BF16)
        u_ref[0, ci, h * CHUNK:(h + 1) * CHUNK, :] = s["sol"][:, :DN_DV]
        qk_ref[0, ci, h * CHUNK:(h + 1) * CHUNK, :] = (s["aq"][CHUNK:] * s["decay"]).astype(BF16)
        kdt_ref[0, ci, h * DN_DK:(h + 1) * DN_DK, :] = kdt.astype(BF16)
        gl_ref[0, ci, h * SUBLANES:(h + 1) * SUBLANES, :] = jnp.broadcast_to(
            jnp.exp(g_last), (SUBLANES, LANES))


def _delta_prep(qkvz, ba, alog, dtb, l):
    batch, seq, _ = qkvz.shape
    n = seq // CHUNK
    rows = _PREP_CHUNKS * CHUNK
    hw = DN_HEADS * DN_DK

    def col_spec(cb):
        return pl.BlockSpec((1, rows, hw), lambda b, i: (b, i, cb))

    def out_spec(r, c):
        return pl.BlockSpec((1, _PREP_CHUNKS, r, c), lambda b, i: (b, i, 0, 0))

    return pl.pallas_call(
        _delta_prep_kernel,
        grid=(batch, n // _PREP_CHUNKS),
        in_specs=[col_spec(0), col_spec(1), col_spec(2),
                  pl.BlockSpec((1, rows, LANES), lambda b, i: (b, i, 0)),
                  _pick(alog, l), _pick(dtb, l)],
        out_specs=[out_spec(DN_HEADS * 2 * CHUNK, DN_DK), out_spec(DN_HEADS * CHUNK, DN_DV),
                   out_spec(DN_HEADS * CHUNK, CHUNK), out_spec(DN_HEADS * DN_DK, CHUNK),
                   out_spec(DN_HEADS * SUBLANES, LANES)],
        out_shape=[jax.ShapeDtypeStruct((batch, n, DN_HEADS * 2 * CHUNK, DN_DK), BF16),
                   jax.ShapeDtypeStruct((batch, n, DN_HEADS * CHUNK, DN_DV), F32),
                   jax.ShapeDtypeStruct((batch, n, DN_HEADS * CHUNK, CHUNK), BF16),
                   jax.ShapeDtypeStruct((batch, n, DN_HEADS * DN_DK, CHUNK), BF16),
                   jax.ShapeDtypeStruct((batch, n, DN_HEADS * SUBLANES, LANES), F32)],
        compiler_params=_params("parallel", "parallel"),
        name="delta_prep",
    )(qkvz, qkvz, qkvz, ba, alog, dtb)


def _delta_scan_kernel(batch, wq_ref, u_ref, qk_ref, kdt_ref, gl_ref, z_ref, nw_ref, o_ref, state_ref):
    c = pl.program_id(0)

    @pl.when(c == 0)
    def _():
        state_ref[...] = jnp.zeros_like(state_ref)

    chains = [(b, h) for b in range(batch) for h in range(DN_HEADS)]
    for cc in range(_SCAN_CHUNKS):
        rs = slice(cc * CHUNK, (cc + 1) * CHUNK)
        r = [_dot(wq_ref[b, cc, h * 2 * CHUNK:(h + 1) * 2 * CHUNK, :],
                  state_ref[b * DN_HEADS + h].astype(BF16)) for b, h in chains]
        v_new = [(u_ref[b, cc, h * CHUNK:(h + 1) * CHUNK, :] - r[i][:CHUNK]).astype(BF16)
                 for i, (b, h) in enumerate(chains)]
        kv = [_dot(kdt_ref[b, cc, h * DN_DK:(h + 1) * DN_DK, :], v_new[i]) for i, (b, h) in enumerate(chains)]
        qv = [_dot(qk_ref[b, cc, h * CHUNK:(h + 1) * CHUNK, :], v_new[i]) for i, (b, h) in enumerate(chains)]
        for i, (b, h) in enumerate(chains):
            state_ref[b * DN_HEADS + h] = (
                state_ref[b * DN_HEADS + h] * gl_ref[b, cc, h * SUBLANES:h * SUBLANES + 1, :] + kv[i])
            o = r[i][CHUNK:] + qv[i]
            z = z_ref[b, rs, h * DN_DV:(h + 1) * DN_DV].astype(F32)
            o = o * lax.rsqrt(jnp.mean(o * o, axis=-1, keepdims=True) + NORM_EPS)
            o = o * nw_ref[...] * (z * _sigmoid(z))
            o_ref[b, rs, h * DN_DV:(h + 1) * DN_DV] = o.astype(o_ref.dtype)


def _delta_scan(wq, u, qk, kdt, gl, qkvz, nw, l):
    batch, seq, _ = qkvz.shape
    hw = DN_HEADS * DN_DV

    rows = _SCAN_CHUNKS * CHUNK

    def step_spec(a):
        return pl.BlockSpec((batch, _SCAN_CHUNKS) + a.shape[2:], lambda c: (0, c, 0, 0))

    return pl.pallas_call(
        functools.partial(_delta_scan_kernel, batch),
        grid=(seq // rows,),
        in_specs=[step_spec(wq), step_spec(u), step_spec(qk), step_spec(kdt), step_spec(gl),
                  pl.BlockSpec((batch, rows, hw), lambda c: (0, c, 3)),
                  _pick(nw, l)],
        out_specs=pl.BlockSpec((batch, rows, hw), lambda c: (0, c, 0)),
        out_shape=jax.ShapeDtypeStruct((batch, seq, hw), BF16),
        scratch_shapes=[pltpu.VMEM((batch * DN_HEADS, DN_DK, DN_DV), F32)],
        compiler_params=_params("arbitrary"),
        name="delta_scan",
    )(wq, u, qk, kdt, gl, qkvz, nw)


def _lru_stages(xg, gates, lam, state, out):
    rows = xg.shape[0]
    xc = xg[:, :LRU_W].astype(F32)
    r = _sigmoid(gates[:, :LRU_W])
    ig = _sigmoid(gates[:, LRU_W:])
    yield
    log_a = -LRU_C * r * _softplus(-lam)
    a = jnp.exp(log_a)
    u = xc * ig * jnp.sqrt(-jnp.tanh(log_a) * (1.0 + a * a))
    yield
    row = lax.broadcasted_iota(jnp.int32, a.shape, 0)
    d = 1
    while d < rows:
        a_sh = jnp.where(row >= d, pltpu.roll(a, d, axis=0), 1.0)
        u_sh = jnp.where(row >= d, pltpu.roll(u, d, axis=0), 0.0)
        u = a * u_sh + u
        a = a * a_sh
        d *= 2
        yield
    h = u + a * state[0]
    state[0] = h[rows - 1:rows, :]
    out.append((h * xg[:, LRU_W:].astype(F32)).astype(BF16))


_QB = 4 * CHUNK
_KB = 3
_CHUNK_SHIFT = CHUNK.bit_length() - 1


def _band_kernel(q_ref, k0_ref, k1_ref, k2_ref, v0_ref, v1_ref, v2_ref, rel_ref, o_ref, bias_ref):
    i = pl.program_id(1)

    @pl.when((pl.program_id(0) == 0) & (i == 0))
    def _():
        qpos = lax.broadcasted_iota(jnp.int32, (_QB, _QB), 0)
        kcol = lax.broadcasted_iota(jnp.int32, (_QB, _QB), 1)
        for m in range(_KB):
            chunk_off = ((kcol + (m - (_KB - 1)) * _QB) >> _CHUNK_SHIFT) - (qpos >> _CHUNK_SHIFT)
            valid = (chunk_off <= 0) & (chunk_off >= -LA_PAST)
            for h in range(LA_HEADS):
                ev = jnp.broadcast_to(rel_ref[h, m:m + 1, :], (_QB, 2 * _QB))
                toeplitz = pltpu.roll(ev, _QB, axis=1, stride=1, stride_axis=0)[:, :_QB]
                bias_ref[h, m] = jnp.where(valid, toeplitz, NEG_INF)

    k_refs = (k0_ref, k1_ref, k2_ref)
    v_refs = (v0_ref, v1_ref, v2_ref)
    lane = lax.broadcasted_iota(jnp.int32, (1, LANES), 1)
    half_sel = (lane < LA_HD, lane >= LA_HD)

    def attend(ms):
        def scores(h):
            ps = slice((h // 2) * LANES, (h // 2 + 1) * LANES)
            qh = jnp.where(half_sel[h % 2], q_ref[0, :, ps], jnp.zeros((), BF16))
            return [_dot_nt(qh, k_refs[m][0, :, ps]) for m in ms]

        s_next = scores(0)
        pair_out = None
        for h in range(LA_HEADS):
            ps = slice((h // 2) * LANES, (h // 2 + 1) * LANES)
            sel = half_sel[h % 2]
            s = s_next
            if h + 1 < LA_HEADS:
                s_next = scores(h + 1)
            s = [sm + bias_ref[h, m] for sm, m in zip(s, ms)]
            mx = s[0].max(axis=-1, keepdims=True)
            for sm in s[1:]:
                mx = jnp.maximum(mx, sm.max(axis=-1, keepdims=True))
            acc = None
            for sm, m in zip(s, ms):
                vh = jnp.where(sel, v_refs[m][0, :, ps], jnp.ones((), BF16))
                part = _dot(jnp.exp((sm - mx).astype(BF16)), vh)
                acc = part if acc is None else acc + part
            den = pltpu.roll(acc, LA_HD, axis=1)
            out = jnp.where(sel, acc / den, 0.0)
            if h % 2 == 0:
                pair_out = out
            else:
                o_ref[0, :, ps] = (pair_out + out).astype(o_ref.dtype)

    for nvalid in range(1, _KB + 1):
        pl.when(jnp.minimum(i, _KB - 1) == nvalid - 1)(
            functools.partial(attend, tuple(range(_KB - nvalid, _KB))))


def _band_attention(qkv, rel_rows, l, *, batch, seq):
    hw = LA_HEADS * LA_HD
    nblk = seq // _QB

    def kv_spec(col, m):
        return pl.BlockSpec((1, _QB, hw), lambda b, i: (b, jnp.maximum(i - (_KB - 1 - m), 0), col))

    return pl.pallas_call(
        _band_kernel,
        grid=(batch, nblk),
        in_specs=[pl.BlockSpec((1, _QB, hw), lambda b, i: (b, i, 0))]
        + [kv_spec(1, m) for m in range(_KB)]
        + [kv_spec(2, m) for m in range(_KB)]
        + [_pick(rel_rows, l)],
        out_specs=pl.BlockSpec((1, _QB, hw), lambda b, i: (b, i, 0)),
        out_shape=jax.ShapeDtypeStruct((batch, seq, hw), BF16),
        scratch_shapes=[pltpu.VMEM((LA_HEADS, _KB, _QB, _QB), F32)],
        compiler_params=_params("arbitrary", "arbitrary"),
        name="band_attention",
    )(qkv, qkv, qkv, qkv, qkv, qkv, qkv, rel_rows)


def _band_rel_rows(rel_table):
    t = rel_table.astype(F32)
    lo = _KB * _QB - REL_CLIP
    hi = _QB - 1 - REL_CLIP
    full = jnp.concatenate([jnp.broadcast_to(t[..., :1], t.shape[:-1] + (lo,)), t,
                            jnp.broadcast_to(t[..., -1:], t.shape[:-1] + (hi,))], axis=-1)
    return jnp.stack([full[..., m * _QB:(m + 2) * _QB] for m in range(_KB)], axis=-2)


_SUB_TILES = 2
_LRU_STAGES_PER_DOT = 2


def _sub_tiles(rows):
    step = rows // _SUB_TILES
    return [slice(s * step, (s + 1) * step) for s in range(_SUB_TILES)]


def _mix_out_kernel(tiles_per_seq, ya_ref, xg_ref, yc_ref, xb_ref, x_ref, wg_ref, gb_ref, lam_ref,
                    wgl_ref, wb_ref, wo_ref, g_ref, b_ref, o_ref, ob_ref, h_ref):
    subs = _sub_tiles(x_ref.shape[0])
    lru_pre = [_dot(xg_ref[rs, :LRU_W], wg_ref[...]) for rs in subs]
    first = (pl.program_id(0) % tiles_per_seq) == 0
    state = [jnp.where(first, 0.0, h_ref[...])]
    yb = []

    def lru_all():
        for rs, pre in zip(subs, lru_pre):
            yield from _lru_stages(xg_ref[rs, :], pre + gb_ref[...], lam_ref[...], state, yb)

    lru = lru_all()
    up_a, up_c, gl = [], [], [[] for _ in subs]
    free_dots = ([lambda rs=rs: up_a.append(_dot(ya_ref[rs, :], wb_ref[0])) for rs in subs]
                 + [lambda rs=rs: up_c.append(_dot(yc_ref[rs, :], wb_ref[2])) for rs in subs]
                 + [lambda s=s, rs=rs, r=r: gl[s].append(
                     _dot_nt(xb_ref[rs, :], wgl_ref[r * D_MODEL:(r + 1) * D_MODEL, :]))
                    for s, rs in enumerate(subs) for r in range(N_BRANCH)])
    for issue in free_dots:
        issue()
        for _ in range(_LRU_STAGES_PER_DOT):
            next(lru, None)
    for _ in lru:
        pass
    h_ref[...] = state[0]
    up_b = [_dot(y, wb_ref[1]) for y in yb]
    merged = []
    for s in range(len(subs)):
        m = (_sigmoid(gl[s][0]) * up_a[s] + _sigmoid(gl[s][1]) * up_b[s]) + _sigmoid(gl[s][2]) * up_c[s]
        merged.append(m.astype(BF16))
    y = [_dot(m, wo_ref[...]) for m in merged]
    for rs, yy in zip(subs, y):
        out = _layer_norm(ALPHA * x_ref[rs, :] + yy, g_ref[...], b_ref[...])
        o_ref[rs, :] = out
        ob_ref[rs, :] = out.astype(BF16)


def _mix_out(ya, xg, yc, xb, x, wg, gb, lam, wgl, wb, wo, ln_g, ln_b, l, *, seq, tm=512):
    t, d = x.shape
    return pl.pallas_call(
        functools.partial(_mix_out_kernel, seq // tm),
        grid=(t // tm,),
        in_specs=[
            pl.BlockSpec((tm, BR_W), lambda i: (i, 0)),
            pl.BlockSpec((tm, 2 * LRU_W), lambda i: (i, 0)),
            pl.BlockSpec((tm, BR_W), lambda i: (i, 0)),
            pl.BlockSpec((tm, d), lambda i: (i, 0)),
            pl.BlockSpec((tm, d), lambda i: (i, 0)),
            _pick(wg, l), _pick(gb, l), _pick(lam, l),
            _pick(wgl, l), _pick(wb, l), _pick(wo, l), _pick(ln_g, l, 1), _pick(ln_b, l, 1),
        ],
        out_specs=[pl.BlockSpec((tm, d), lambda i: (i, 0)), pl.BlockSpec((tm, d), lambda i: (i, 0))],
        out_shape=[jax.ShapeDtypeStruct((t, d), F32), jax.ShapeDtypeStruct((t, d), BF16)],
        scratch_shapes=[pltpu.VMEM((1, LRU_W), F32)],
        compiler_params=_params("arbitrary"),
        name="mix_out",
    )(ya, xg, yc, xb, x, wg, gb, lam, wgl, wb, wo, ln_g, ln_b)


def _xattn_kernel(xb_ref, x_ref, kv_ref, wq_ref, wo_ref, g_ref, b_ref, o_ref, ob_ref):
    subs = _sub_tiles(x_ref.shape[1])
    q = [(_dot(xb_ref[0, rs, :], wq_ref[...]) * np.float32(XA_HD ** -0.5)).astype(BF16) for rs in subs]

    def scores(task):
        s, h = task
        sl = slice(h * XA_HD, (h + 1) * XA_HD)
        return _dot_nt(q[s][:, sl], kv_ref[0, :, sl])

    tasks = [(s, h) for h in range(XA_HEADS) for s in range(len(subs))]
    outs = [[None] * XA_HEADS for _ in subs]
    s_next = scores(tasks[0])
    for ti, (s, h) in enumerate(tasks):
        v = kv_ref[0, :, D_MODEL + h * XA_HD:D_MODEL + (h + 1) * XA_HD]
        sc = s_next
        if ti + 1 < len(tasks):
            s_next = scores(tasks[ti + 1])
        p = jnp.exp(sc - sc.max(axis=-1, keepdims=True))
        den = p.sum(axis=-1, keepdims=True)
        outs[s][h] = (_dot(p.astype(BF16), v) / den).astype(BF16)
    y = [_dot(jnp.concatenate(o, axis=1), wo_ref[...]) for o in outs]
    for rs, yy in zip(subs, y):
        out = _layer_norm(ALPHA * x_ref[0, rs, :] + yy, g_ref[...], b_ref[...])
        o_ref[0, rs, :] = out
        ob_ref[0, rs, :] = out.astype(BF16)


def _xattn(xb, x, kv, wq, wo, ln_g, ln_b, l, *, tm=512):
    batch, seq, d = x.shape
    return pl.pallas_call(
        _xattn_kernel,
        grid=(batch, seq // tm),
        in_specs=[
            pl.BlockSpec((1, tm, d), lambda bi, i: (bi, i, 0)),
            pl.BlockSpec((1, tm, d), lambda bi, i: (bi, i, 0)),
            pl.BlockSpec((1, MEM_LEN, 2 * d), lambda bi, i: (bi, 0, 0)),
            _pick(wq, l), _pick(wo, l), _pick(ln_g, l, 2), _pick(ln_b, l, 2),
        ],
        out_specs=[
            pl.BlockSpec((1, tm, d), lambda bi, i: (bi, i, 0)),
            pl.BlockSpec((1, tm, d), lambda bi, i: (bi, i, 0)),
        ],
        out_shape=[jax.ShapeDtypeStruct((batch, seq, d), F32), jax.ShapeDtypeStruct((batch, seq, d), BF16)],
        compiler_params=_params("parallel", "parallel"),
        name="mem_xattn",
    )(xb, x, kv, wq, wo, ln_g, ln_b)


def _prepare(p):
    d = D_MODEL
    depth = p["ln_g"].shape[0]
    wt = jnp.swapaxes(p["mix_w_in"].astype(F32), 1, 2)
    hw = LA_HEADS * LA_HD
    w_packed = jnp.concatenate([
        wt[:, :_O_DB],
        jnp.pad(wt[:, _O_DB:_O_LX], ((0, 0), (0, LANES - 2 * DN_HEADS), (0, 0))),
        wt[:, _O_LX:_O_AQ],
        wt[:, _O_AQ:_O_AQ + hw] * np.float32(LA_HD ** -0.5),
        wt[:, _O_AQ + hw:_O_GL]], axis=1).astype(BF16)
    w_packed = jnp.swapaxes(w_packed, 1, 2)
    wgl = wt[:, _O_GL:].astype(BF16)

    def lane_row(v):
        return jnp.pad(v.astype(F32), ((0, 0), (DN_HEADS, LANES - 2 * DN_HEADS)))[:, None, :]

    eye = jnp.eye(LRU_BLOCKS, dtype=F32)
    wg = jnp.einsum("lgnde,nm->lgndme", p["lru_gate_w"].astype(F32), eye)
    wg = wg.reshape(depth, 2, LRU_W, LRU_W)
    wg = jnp.concatenate([wg[:, 0], wg[:, 1]], axis=2).astype(BF16)
    return dict(
        ln_g=p["ln_g"].astype(F32).reshape(depth, 4, 1, d),
        ln_b=p["ln_b"].astype(F32).reshape(depth, 4, 1, d),
        w12=p["ffn_w12"].astype(BF16), w3=p["ffn_w3"].astype(BF16),
        w_in=w_packed, wgl=wgl,
        dcw=p["dn_conv_w"].astype(F32), lcw=p["lru_conv_w"].astype(F32),
        lcb=p["lru_conv_b"].astype(F32)[:, None, :],
        alog=lane_row(p["dn_a_log"]), dtb=lane_row(p["dn_dt_bias"]),
        nw=p["dn_norm_w"].astype(F32)[:, None, :],
        wg=wg, gb=p["lru_gate_b"].astype(F32).reshape(depth, 1, 2 * LRU_W),
        lam=p["lru_lambda"].astype(F32)[:, None, :],
        rel=_band_rel_rows(p["la_rel_bias"]),
        wb=p["w_branch"].astype(BF16), wout=p["mix_w_out"].astype(BF16),
        wq=p["xa_wq"].astype(BF16), wkv=p["xa_wkv"].astype(BF16), wo=p["xa_wo"].astype(BF16))


def _layer(x, mem_b, l, w):
    batch, seq, d = x.shape
    t = batch * seq

    x, xb = _ffn(x.reshape(t, d), w["w12"], w["w3"], w["ln_g"], w["ln_b"], (l, 0))

    qkvz, ba, xg, aqkv = _mixer_in(xb, w["w_in"], w["dcw"], w["lcw"], w["lcb"], l, seq=seq)
    qkvz = qkvz.reshape(batch, seq, -1)
    factors = _delta_prep(qkvz, ba.reshape(batch, seq, LANES), w["alog"], w["dtb"], l)
    ya = _delta_scan(*factors, qkvz, w["nw"], l)
    yc = _band_attention(aqkv.reshape(batch, seq, -1), w["rel"], l, batch=batch, seq=seq)
    x, xb = _mix_out(ya.reshape(t, -1), xg, yc.reshape(t, -1), xb, x, w["wg"], w["gb"], w["lam"],
                     w["wgl"], w["wb"], w["wout"], w["ln_g"], w["ln_b"], l, seq=seq)

    kv = _proj_plain(mem_b, w["wkv"], l, tn=D_MODEL, out_dtype=BF16, name="proj_kv")
    x, xb = _xattn(xb.reshape(batch, seq, d), x.reshape(batch, seq, d), kv.reshape(batch, MEM_LEN, 2 * d),
                   w["wq"], w["wo"], w["ln_g"], w["ln_b"], l)

    x, _ = _ffn(x.reshape(t, d), w["w12"], w["w3"], w["ln_g"], w["ln_b"], (l, 1))
    return x.reshape(batch, seq, d)


def kernel(x, mem, ln_g, ln_b, ffn_w12, ffn_w3, mix_w_in, dn_conv_w, dn_a_log, dn_dt_bias, dn_norm_w,
           lru_conv_w, lru_conv_b, lru_gate_w, lru_gate_b, lru_lambda, la_rel_bias, w_branch, mix_w_out,
           xa_wq, xa_wkv, xa_wo):
    w = _prepare(dict(
        ln_g=ln_g, ln_b=ln_b, ffn_w12=ffn_w12, ffn_w3=ffn_w3, mix_w_in=mix_w_in, dn_conv_w=dn_conv_w,
        dn_a_log=dn_a_log, dn_dt_bias=dn_dt_bias, dn_norm_w=dn_norm_w, lru_conv_w=lru_conv_w,
        lru_conv_b=lru_conv_b, lru_gate_w=lru_gate_w, lru_gate_b=lru_gate_b, lru_lambda=lru_lambda,
        la_rel_bias=la_rel_bias, w_branch=w_branch, mix_w_out=mix_w_out, xa_wq=xa_wq, xa_wkv=xa_wkv,
        xa_wo=xa_wo))
    batch = x.shape[0]
    mem_b = mem.reshape(batch * MEM_LEN, D_MODEL).astype(BF16)
    x = x.astype(F32)
    for l in range(DEPTH):
        x = _layer(x, mem_b, l, w)
    return x
```

```python
import functools

import numpy as np
import jax
import jax.numpy as jnp
from jax import lax
from jax.experimental import pallas as pl
from jax.experimental.pallas import tpu as pltpu

F32 = jnp.float32
BF16 = jnp.bfloat16

D_MODEL = 1024
DEPTH = 2
CHUNK = 64
CONV_W = 4
BR_W = 512
N_BRANCH = 3
DN_HEADS = 4
DN_DK = 128
DN_DV = 128
LRU_W = BR_W
LRU_BLOCKS = 8
LRU_BLK = LRU_W // LRU_BLOCKS
LRU_C = 8.0
LA_HEADS = 8
LA_HD = 64
LA_PAST = 8
REL_CLIP = 128
MEM_LEN = 256
XA_HEADS = 4
XA_HD = D_MODEL // XA_HEADS
D_FF = 2816
ALPHA = (2 * DEPTH) ** 0.25
LN_EPS = 1e-5
NORM_EPS = 1e-6
NEG_INF = -1e30

LANES = 128
SUBLANES = 8
VMEM_LIMIT = 56 * 1024 * 1024

_O_DZ = 3 * DN_HEADS * DN_DK
_O_DB = _O_DZ + DN_HEADS * DN_DV
_O_LX = _O_DB + 2 * DN_HEADS
_O_AQ = _O_LX + 2 * LRU_W
_O_GL = _O_AQ + 3 * LA_HEADS * LA_HD


def _params(*sem):
    return pltpu.CompilerParams(dimension_semantics=sem, vmem_limit_bytes=VMEM_LIMIT)


def _dot(a, b):
    return jnp.dot(a, b, preferred_element_type=F32)


def _dot_nt(a, b):
    return lax.dot_general(a, b, (((1,), (1,)), ((), ())), preferred_element_type=F32)


def _sigmoid(x):
    return 1.0 / (1.0 + jnp.exp(-x))


def _softplus(x):
    return jnp.maximum(x, 0.0) + jnp.log1p(jnp.exp(-jnp.abs(x)))


def _gelu_tanh(x):
    c = np.float32(np.sqrt(2.0 / np.pi))
    return 0.5 * x * (1.0 + jnp.tanh(c * (x + np.float32(0.044715) * (x * x * x))))


def _layer_norm(y, g, b):
    mu = jnp.mean(y, axis=-1, keepdims=True)
    d = y - mu
    var = jnp.mean(d * d, axis=-1, keepdims=True)
    return d * lax.rsqrt(var + LN_EPS) * g + b


def _pick(arr, *idx):
    shape = (None,) * len(idx) + tuple(arr.shape[len(idx):])
    index = tuple(idx) + (0,) * (arr.ndim - len(idx))
    return pl.BlockSpec(shape, lambda *_: index, pipeline_mode=pl.Buffered(1))


_FF_CHUNK = 256


def _ffn_kernel(x_ref, w12_ref, w3_ref, g_ref, b_ref, o_ref, ob_ref):
    x = x_ref[...]
    xb = x.astype(BF16)
    n = D_FF // _FF_CHUNK

    def gate_up(c):
        lo = c * _FF_CHUNK
        return (_dot(xb, w12_ref[:, lo:lo + _FF_CHUNK]),
                _dot(xb, w12_ref[:, D_FF + lo:D_FF + lo + _FF_CHUNK]))

    nxt = gate_up(0)
    acc = None
    for c in range(n):
        g, u = nxt
        if c + 1 < n:
            nxt = gate_up(c + 1)
        h = (g * _sigmoid(g) * u).astype(BF16)
        part = _dot(h, w3_ref[c * _FF_CHUNK:(c + 1) * _FF_CHUNK, :])
        acc = part if acc is None else acc + part
    out = _layer_norm(ALPHA * x + 0.5 * acc, g_ref[...], b_ref[...])
    o_ref[...] = out
    ob_ref[...] = out.astype(BF16)


def _ffn(x, w12, w3, ln_g, ln_b, sel, *, tm=1024):
    t, d = x.shape
    l, k = sel
    return pl.pallas_call(
        _ffn_kernel,
        grid=(t // tm,),
        in_specs=[
            pl.BlockSpec((tm, d), lambda i: (i, 0)),
            _pick(w12, l, k),
            _pick(w3, l, k),
            _pick(ln_g, l, 3 * k),
            _pick(ln_b, l, 3 * k),
        ],
        out_specs=[
            pl.BlockSpec((tm, d), lambda i: (i, 0)),
            pl.BlockSpec((tm, d), lambda i: (i, 0)),
        ],
        out_shape=[jax.ShapeDtypeStruct((t, d), F32), jax.ShapeDtypeStruct((t, d), BF16)],
        compiler_params=_params("parallel"),
        name="ffn_ln",
    )(x, w12, w3, ln_g, ln_b)


_HW = DN_HEADS * DN_DK
_C_QKV = 0
_C_Z = 3 * _HW
_C_BA = _C_Z + _HW
_C_LX = _C_BA + LANES
_C_LG = _C_LX + LRU_W
_C_ATT = _C_LG + LRU_W
_SEG = 256
_N_TAILS = (3 * _HW + LRU_W) // _SEG


def _causal_conv(y, tail, cw):
    row8 = lax.broadcasted_iota(jnp.int32, (SUBLANES, y.shape[1]), 0)
    acc = y * cw[CONV_W - 1:CONV_W, :]
    fix = jnp.zeros((SUBLANES, y.shape[1]), F32)
    for k in range(1, CONV_W):
        wk = cw[CONV_W - 1 - k:CONV_W - k, :]
        sh = pltpu.roll(y, k, axis=0)
        acc = acc + sh * wk
        prev = pltpu.roll(tail, k, axis=0)
        fix = fix + jnp.where(row8 < k, (prev - sh[:SUBLANES]) * wk, 0.0)
    return jnp.concatenate([acc[:SUBLANES] + fix, acc[SUBLANES:]], axis=0)


def _mixer_in_kernel(tiles_per_seq, x_ref, w_ref, dcw_ref, lcw_ref, lcb_ref,
                     qkvz_ref, ba_ref, xg_ref, att_ref, tail_ref):
    xb = x_ref[...]
    first = (pl.program_id(0) % tiles_per_seq) == 0
    rows = xb.shape[0]

    def proj(c0, width=_SEG):
        return _dot(xb, w_ref[:, c0:c0 + width])

    def cols(j):
        return slice(j * _SEG, (j + 1) * _SEG)

    def conv(y, slot, cw):
        tail = jnp.where(first, 0.0, tail_ref[slot])
        tail_ref[slot] = y[rows - SUBLANES:]
        return _causal_conv(y, tail, cw)

    def delta_qkv(j, y):
        c = conv(y, j, dcw_ref[:, cols(j)])
        c = c * _sigmoid(c)
        if j * _SEG < 2 * _HW:
            scale = np.float32(DN_DK ** -0.5 if j * _SEG < _HW else 1.0)
            parts = []
            for h in range(_SEG // DN_DK):
                ch = c[:, h * DN_DK:(h + 1) * DN_DK]
                parts.append(ch * (lax.rsqrt(jnp.sum(ch * ch, axis=-1, keepdims=True) + NORM_EPS) * scale))
            c = jnp.concatenate(parts, axis=1)
        qkvz_ref[:, cols(j)] = c.astype(BF16)

    def z_part(j):
        qkvz_ref[:, 3 * _HW + j * _SEG:3 * _HW + (j + 1) * _SEG] = proj(_C_Z + j * _SEG).astype(BF16)

    def lru_x(j, y):
        slot = 3 * _HW // _SEG + j
        xg_ref[:, cols(j)] = (conv(y, slot, lcw_ref[:, cols(j)]) + lcb_ref[:, cols(j)]).astype(BF16)

    def lru_g(j, y):
        xg_ref[:, LRU_W + j * _SEG:LRU_W + (j + 1) * _SEG] = _gelu_tanh(y).astype(BF16)

    def att(j):
        att_ref[:, cols(j)] = proj(_C_ATT + j * _SEG).astype(BF16)

    heavy = ([(_C_QKV + j * _SEG, functools.partial(delta_qkv, j)) for j in range(3 * _HW // _SEG)]
             + [(_C_LX + j * _SEG, functools.partial(lru_x, j)) for j in range(LRU_W // _SEG)]
             + [(_C_LG + j * _SEG, functools.partial(lru_g, j)) for j in range(LRU_W // _SEG)])
    light = ([functools.partial(att, j) for j in range(3 * _HW // _SEG)]
             + [functools.partial(z_part, j) for j in range(_HW // _SEG)])
    for i, (c0, epilogue) in enumerate(heavy):
        epilogue(proj(c0))
        if i < len(light):
            light[i]()
    for g in light[len(heavy):]:
        g()
    ba_ref[...] = proj(_C_BA, LANES)


def _mixer_in(xb, w, dcw, lcw, lcb, l, *, seq, tm=512):
    t, d = xb.shape

    def tile(width):
        return pl.BlockSpec((tm, width), lambda i: (i, 0))

    widths = (4 * _HW, LANES, 2 * LRU_W, 3 * _HW)
    dtypes = (BF16, F32, BF16, BF16)
    return pl.pallas_call(
        functools.partial(_mixer_in_kernel, seq // tm),
        grid=(t // tm,),
        in_specs=[tile(d), _pick(w, l), _pick(dcw, l), _pick(lcw, l), _pick(lcb, l)],
        out_specs=[tile(wd) for wd in widths],
        out_shape=[jax.ShapeDtypeStruct((t, wd), dt) for wd, dt in zip(widths, dtypes)],
        scratch_shapes=[pltpu.VMEM((_N_TAILS, SUBLANES, _SEG), F32)],
        compiler_params=_params("arbitrary"),
        name="mixer_in",
    )(xb, w, dcw, lcw, lcb)


def _proj_plain_kernel(x_ref, w_ref, o_ref):
    o_ref[...] = _dot(x_ref[...], w_ref[...]).astype(o_ref.dtype)


def _proj_plain(xb, w, l, *, tn, out_dtype, name):
    t, d = xb.shape
    n = w.shape[2]
    return pl.pallas_call(
        _proj_plain_kernel,
        grid=(n // tn,),
        in_specs=[pl.BlockSpec((t, d), lambda j: (0, 0)), pl.BlockSpec((None, d, tn), lambda j: (l, 0, j))],
        out_specs=pl.BlockSpec((t, tn), lambda j: (0, j)),
        out_shape=jax.ShapeDtypeStruct((t, n), out_dtype),
        compiler_params=_params("parallel"),
        name=name,
    )(xb, w)


_DELTA_CHUNKS = 4
_PREP_YIELD = 8
_SCAN_HEAD = 6
_SCAN_EVERY = 8
_FACTOR_SHAPES = ((DN_HEADS * 2 * CHUNK, DN_DK, BF16),
                  (DN_HEADS * CHUNK, DN_DV, F32),
                  (DN_HEADS * CHUNK, CHUNK, BF16),
                  (DN_HEADS * DN_DK, CHUNK, BF16),
                  (DN_HEADS * SUBLANES, LANES, F32))


def _segmented_cumsum_rows(x, seg):
    row = lax.broadcasted_iota(jnp.int32, x.shape, 0) & (seg - 1)
    d = 1
    while d < seg:
        x = x + jnp.where(row >= d, pltpu.roll(x, d, axis=0), 0.0)
        d *= 2
    return x


def _delta_prep_stages(batch, slot, q_ref, k_ref, v_ref, ba_ref, alog_ref, dtb_ref, factors):
    wq_s, u_s, qk_s, kdt_s, gl_s = factors
    row = lax.broadcasted_iota(jnp.int32, (CHUNK, CHUNK), 0)
    col = lax.broadcasted_iota(jnp.int32, (CHUNK, CHUNK), 1)
    lower = row >= col
    strict = row > col
    eye = jnp.where(row == col, 1.0, 0.0).astype(F32)
    zpad = jnp.zeros((LANES - CHUNK, LANES), F32)

    beta_all, g_cum, g_cum_t = [], [], []
    for b in range(batch):
        ba = ba_ref[b]
        beta_all.append(_sigmoid(ba))
        g = _segmented_cumsum_rows(-jnp.exp(alog_ref[...]) * _softplus(ba + dtb_ref[...]), CHUNK)
        g_cum.append(g)
        g_cum_t.append([jnp.transpose(g[pr * LANES:(pr + 1) * LANES]) for pr in range(_DELTA_CHUNKS // 2)])

    inst = [(b, ci, h) for b in range(batch) for ci in range(_DELTA_CHUNKS) for h in range(DN_HEADS)]

    def paced(items):
        for n, item in enumerate(items):
            yield item
            if n % _PREP_YIELD == _PREP_YIELD - 1:
                yield None

    st = []
    for it in paced(inst):
        if it is None:
            yield
            continue
        b, ci, h = it
        rs = slice(ci * CHUNK, (ci + 1) * CHUNK)
        hs = slice(h * DN_DK, (h + 1) * DN_DK)
        q = q_ref[b, rs, hs].astype(F32)
        k = k_ref[b, rs, hs].astype(F32)
        beta = beta_all[b][rs, h:h + 1]
        gc = g_cum[b][rs, DN_HEADS + h:DN_HEADS + h + 1]
        gr = g_cum_t[b][ci // 2][DN_HEADS + h:DN_HEADS + h + 1, (ci % 2) * CHUNK:(ci % 2 + 1) * CHUNK]
        decay = jnp.exp(jnp.where(lower, gc - gr, NEG_INF))
        kb = k * beta
        aq = _dot_nt(jnp.concatenate([kb, q], axis=0).astype(BF16), k.astype(BF16))
        st.append(dict(b=b, ci=ci, h=h, rs=rs, hs=hs, beta=beta, gc=gc, kb=kb, aq=aq, decay=decay))

    for s in st:
        s["p"] = -jnp.where(strict, s["aq"][:CHUNK] * s["decay"], 0.0)
        s["tinv"] = eye + s["p"]
    n = 1
    while 2 * n < CHUNK:
        for s in paced(st):
            if s is None:
                yield
                continue
            p16 = s["p"].astype(BF16)
            s["p"] = _dot(p16, p16)
        for s in paced(st):
            if s is None:
                yield
                continue
            s["tinv"] = s["tinv"] + _dot(s["tinv"].astype(BF16), s["p"].astype(BF16))
        n *= 2

    for s in paced(st):
        if s is None:
            yield
            continue
        exp_g = jnp.exp(s["gc"])
        v = v_ref[s["b"], s["rs"], s["hs"]].astype(F32)
        rhs = jnp.concatenate([v * s["beta"], s["kb"] * exp_g], axis=1)
        s["sol"] = _dot(s["tinv"].astype(BF16), rhs.astype(BF16))
        s["exp_g"] = exp_g

    for s in paced(st):
        if s is None:
            yield
            continue
        b, ci, h = s["b"], s["ci"], s["h"]
        c = b * _DELTA_CHUNKS + ci
        q = q_ref[b, s["rs"], s["hs"]].astype(F32)
        k = k_ref[b, s["rs"], s["hs"]].astype(F32)
        g_last = s["gc"][CHUNK - 1:CHUNK, :]
        k_dec = k * jnp.exp(g_last - s["gc"])
        kdt = jnp.transpose(jnp.concatenate([k_dec, zpad], axis=0))[:, :CHUNK]
        wq_s[slot, c, h * 2 * CHUNK:(h + 1) * 2 * CHUNK, :] = jnp.concatenate(
            [s["sol"][:, DN_DV:], q * s["exp_g"]], axis=0).astype(BF16)
        u_s[slot, c, h * CHUNK:(h + 1) * CHUNK, :] = s["sol"][:, :DN_DV]
        qk_s[slot, c, h * CHUNK:(h + 1) * CHUNK, :] = (s["aq"][CHUNK:] * s["decay"]).astype(BF16)
        kdt_s[slot, c, h * DN_DK:(h + 1) * DN_DK, :] = kdt.astype(BF16)
        gl_s[slot, c, h * SUBLANES:(h + 1) * SUBLANES, :] = jnp.broadcast_to(
            jnp.exp(g_last), (SUBLANES, LANES))


def _delta_scan_stages(batch, slot, factors, z_ref, nw_ref, o_ref, state_ref):
    wq_s, u_s, qk_s, kdt_s, gl_s = factors
    chains = [(b, h) for b in range(batch) for h in range(DN_HEADS)]
    for cc in range(_DELTA_CHUNKS):
        rs = slice(cc * CHUNK, (cc + 1) * CHUNK)
        r = [_dot(wq_s[slot, b * _DELTA_CHUNKS + cc, h * 2 * CHUNK:(h + 1) * 2 * CHUNK, :],
                  state_ref[b * DN_HEADS + h].astype(BF16)) for b, h in chains]
        yield
        v_new = [(u_s[slot, b * _DELTA_CHUNKS + cc, h * CHUNK:(h + 1) * CHUNK, :] - r[i][:CHUNK]).astype(BF16)
                 for i, (b, h) in enumerate(chains)]
        kv = [_dot(kdt_s[slot, b * _DELTA_CHUNKS + cc, h * DN_DK:(h + 1) * DN_DK, :], v_new[i])
              for i, (b, h) in enumerate(chains)]
        qv = [_dot(qk_s[slot, b * _DELTA_CHUNKS + cc, h * CHUNK:(h + 1) * CHUNK, :], v_new[i])
              for i, (b, h) in enumerate(chains)]
        yield
        for i, (b, h) in enumerate(chains):
            state_ref[b * DN_HEADS + h] = (
                state_ref[b * DN_HEADS + h]
                * gl_s[slot, b * _DELTA_CHUNKS + cc, h * SUBLANES:h * SUBLANES + 1, :] + kv[i])
            o = r[i][CHUNK:] + qv[i]
            z = z_ref[b, rs, h * DN_DV:(h + 1) * DN_DV].astype(F32)
            o = o * lax.rsqrt(jnp.mean(o * o, axis=-1, keepdims=True) + NORM_EPS)
            o = o * nw_ref[...] * (z * _sigmoid(z))
            o_ref[b, rs, h * DN_DV:(h + 1) * DN_DV] = o.astype(o_ref.dtype)


def _delta_kernel(batch, q_ref, k_ref, v_ref, ba_ref, z_ref, alog_ref, dtb_ref, nw_ref, o_ref,
                  wq_s, u_s, qk_s, kdt_s, gl_s, state_ref):
    i = pl.program_id(0)
    factors = (wq_s, u_s, qk_s, kdt_s, gl_s)

    @pl.when(i == 0)
    def _():
        state_ref[...] = jnp.zeros_like(state_ref)
        for f in factors:
            f[...] = jnp.zeros_like(f)

    cur = i % 2
    prep = _delta_prep_stages(batch, cur, q_ref, k_ref, v_ref, ba_ref, alog_ref, dtb_ref, factors)
    scan = _delta_scan_stages(batch, 1 - cur, factors, z_ref, nw_ref, o_ref, state_ref)
    for _ in range(_SCAN_HEAD):
        next(scan, None)
    for n, _ in enumerate(prep):
        if n % _SCAN_EVERY == 0:
            next(scan, None)
    for _ in scan:
        pass


def _delta(qkvz, ba, alog, dtb, nw, l):
    batch, seq, _ = qkvz.shape
    rows = _DELTA_CHUNKS * CHUNK
    nb = seq // rows
    hw = DN_HEADS * DN_DK

    def cur_spec(width, cb):
        return pl.BlockSpec((batch, rows, width), lambda i: (0, jnp.minimum(i, nb - 1), cb))

    def prev_spec(cb):
        return pl.BlockSpec((batch, rows, hw), lambda i: (0, jnp.maximum(i - 1, 0), cb))

    return pl.pallas_call(
        functools.partial(_delta_kernel, batch),
        grid=(nb + 1,),
        in_specs=[cur_spec(hw, 0), cur_spec(hw, 1), cur_spec(hw, 2), cur_spec(LANES, 0), prev_spec(3),
                  _pick(alog, l), _pick(dtb, l), _pick(nw, l)],
        out_specs=prev_spec(0),
        out_shape=jax.ShapeDtypeStruct((batch, seq, hw), BF16),
        scratch_shapes=[pltpu.VMEM((2, batch * _DELTA_CHUNKS, r, c), dt) for r, c, dt in _FACTOR_SHAPES]
        + [pltpu.VMEM((batch * DN_HEADS, DN_DK, DN_DV), F32)],
        compiler_params=_params("arbitrary"),
        name="gated_delta",
    )(qkvz, qkvz, qkvz, ba, qkvz, alog, dtb, nw)


def _lru_stages(xg, gates, lam, state, out):
    rows = xg.shape[0]
    xc = xg[:, :LRU_W].astype(F32)
    r = _sigmoid(gates[:, :LRU_W])
    ig = _sigmoid(gates[:, LRU_W:])
    yield
    log_a = -LRU_C * r * _softplus(-lam)
    a = jnp.exp(log_a)
    u = xc * ig * jnp.sqrt(-jnp.tanh(log_a) * (1.0 + a * a))
    yield
    row = lax.broadcasted_iota(jnp.int32, a.shape, 0)
    d = 1
    while d < rows:
        a_sh = jnp.where(row >= d, pltpu.roll(a, d, axis=0), 1.0)
        u_sh = jnp.where(row >= d, pltpu.roll(u, d, axis=0), 0.0)
        u = a * u_sh + u
        a = a * a_sh
        d *= 2
        yield
    h = u + a * state[0]
    state[0] = h[rows - 1:rows, :]
    out.append((h * xg[:, LRU_W:].astype(F32)).astype(BF16))


_QB = 4 * CHUNK
_KB = 3
_CHUNK_SHIFT = CHUNK.bit_length() - 1


def _band_kernel(q_ref, k0_ref, k1_ref, k2_ref, v0_ref, v1_ref, v2_ref, rel_ref, o_ref, bias_ref):
    i = pl.program_id(1)

    @pl.when((pl.program_id(0) == 0) & (i == 0))
    def _():
        qpos = lax.broadcasted_iota(jnp.int32, (_QB, _QB), 0)
        kcol = lax.broadcasted_iota(jnp.int32, (_QB, _QB), 1)
        for m in range(_KB):
            chunk_off = ((kcol + (m - (_KB - 1)) * _QB) >> _CHUNK_SHIFT) - (qpos >> _CHUNK_SHIFT)
            valid = (chunk_off <= 0) & (chunk_off >= -LA_PAST)
            for h in range(LA_HEADS):
                ev = jnp.broadcast_to(rel_ref[h, m:m + 1, :], (_QB, 2 * _QB))
                toeplitz = pltpu.roll(ev, _QB, axis=1, stride=1, stride_axis=0)[:, :_QB]
                bias_ref[h, m] = jnp.where(valid, toeplitz, NEG_INF)

    k_refs = (k0_ref, k1_ref, k2_ref)
    v_refs = (v0_ref, v1_ref, v2_ref)
    lane = lax.broadcasted_iota(jnp.int32, (1, LANES), 1)
    half_sel = (lane < LA_HD, lane >= LA_HD)

    def attend(ms):
        def scores(h):
            ps = slice((h // 2) * LANES, (h // 2 + 1) * LANES)
            qh = jnp.where(half_sel[h % 2], q_ref[0, :, ps], jnp.zeros((), BF16))
            return [_dot_nt(qh, k_refs[m][0, :, ps]) for m in ms]

        s_next = scores(0)
        pair_out = None
        for h in range(LA_HEADS):
            ps = slice((h // 2) * LANES, (h // 2 + 1) * LANES)
            sel = half_sel[h % 2]
            s = s_next
            if h + 1 < LA_HEADS:
                s_next = scores(h + 1)
            s = [sm + bias_ref[h, m] for sm, m in zip(s, ms)]
            mx = s[0].max(axis=-1, keepdims=True)
            for sm in s[1:]:
                mx = jnp.maximum(mx, sm.max(axis=-1, keepdims=True))
            acc = None
            for sm, m in zip(s, ms):
                vh = jnp.where(sel, v_refs[m][0, :, ps], jnp.ones((), BF16))
                part = _dot(jnp.exp((sm - mx).astype(BF16)), vh)
                acc = part if acc is None else acc + part
            den = pltpu.roll(acc, LA_HD, axis=1)
            out = jnp.where(sel, acc / den, 0.0)
            if h % 2 == 0:
                pair_out = out
            else:
                o_ref[0, :, ps] = (pair_out + out).astype(o_ref.dtype)

    for nvalid in range(1, _KB + 1):
        pl.when(jnp.minimum(i, _KB - 1) == nvalid - 1)(
            functools.partial(attend, tuple(range(_KB - nvalid, _KB))))


def _band_attention(qkv, rel_rows, l, *, batch, seq):
    hw = LA_HEADS * LA_HD
    nblk = seq // _QB

    def kv_spec(col, m):
        return pl.BlockSpec((1, _QB, hw), lambda b, i: (b, jnp.maximum(i - (_KB - 1 - m), 0), col))

    return pl.pallas_call(
        _band_kernel,
        grid=(batch, nblk),
        in_specs=[pl.BlockSpec((1, _QB, hw), lambda b, i: (b, i, 0))]
        + [kv_spec(1, m) for m in range(_KB)]
        + [kv_spec(2, m) for m in range(_KB)]
        + [_pick(rel_rows, l)],
        out_specs=pl.BlockSpec((1, _QB, hw), lambda b, i: (b, i, 0)),
        out_shape=jax.ShapeDtypeStruct((batch, seq, hw), BF16),
        scratch_shapes=[pltpu.VMEM((LA_HEADS, _KB, _QB, _QB), F32)],
        compiler_params=_params("arbitrary", "arbitrary"),
        name="band_attention",
    )(qkv, qkv, qkv, qkv, qkv, qkv, qkv, rel_rows)


def _band_rel_rows(rel_table):
    t = rel_table.astype(F32)
    lo = _KB * _QB - REL_CLIP
    hi = _QB - 1 - REL_CLIP
    full = jnp.concatenate([jnp.broadcast_to(t[..., :1], t.shape[:-1] + (lo,)), t,
                            jnp.broadcast_to(t[..., -1:], t.shape[:-1] + (hi,))], axis=-1)
    return jnp.stack([full[..., m * _QB:(m + 2) * _QB] for m in range(_KB)], axis=-2)


_SUB_TILES = 2
_LRU_STAGES_PER_DOT = 2


def _sub_tiles(rows):
    step = rows // _SUB_TILES
    return [slice(s * step, (s + 1) * step) for s in range(_SUB_TILES)]


def _mix_out_kernel(tiles_per_seq, ya_ref, xg_ref, yc_ref, xb_ref, x_ref, wg_ref, gb_ref, lam_ref,
                    wgl_ref, wb_ref, wo_ref, g_ref, b_ref, o_ref, ob_ref, h_ref):
    subs = _sub_tiles(x_ref.shape[0])
    lru_pre = [_dot(xg_ref[rs, :LRU_W], wg_ref[...]) for rs in subs]
    first = (pl.program_id(0) % tiles_per_seq) == 0
    state = [jnp.where(first, 0.0, h_ref[...])]
    yb = []

    def lru_all():
        for rs, pre in zip(subs, lru_pre):
            yield from _lru_stages(xg_ref[rs, :], pre + gb_ref[...], lam_ref[...], state, yb)

    lru = lru_all()
    up_a, up_c, gl = [], [], [[] for _ in subs]
    free_dots = ([lambda rs=rs: up_a.append(_dot(ya_ref[rs, :], wb_ref[0])) for rs in subs]
                 + [lambda rs=rs: up_c.append(_dot(yc_ref[rs, :], wb_ref[2])) for rs in subs]
                 + [lambda s=s, rs=rs, r=r: gl[s].append(
                     _dot_nt(xb_ref[rs, :], wgl_ref[r * D_MODEL:(r + 1) * D_MODEL, :]))
                    for s, rs in enumerate(subs) for r in range(N_BRANCH)])
    for issue in free_dots:
        issue()
        for _ in range(_LRU_STAGES_PER_DOT):
            next(lru, None)
    for _ in lru:
        pass
    h_ref[...] = state[0]
    up_b = [_dot(y, wb_ref[1]) for y in yb]
    merged = []
    for s in range(len(subs)):
        m = (_sigmoid(gl[s][0]) * up_a[s] + _sigmoid(gl[s][1]) * up_b[s]) + _sigmoid(gl[s][2]) * up_c[s]
        merged.append(m.astype(BF16))
    y = [_dot(m, wo_ref[...]) for m in merged]
    for rs, yy in zip(subs, y):
        out = _layer_norm(ALPHA * x_ref[rs, :] + yy, g_ref[...], b_ref[...])
        o_ref[rs, :] = out
        ob_ref[rs, :] = out.astype(BF16)


def _mix_out(ya, xg, yc, xb, x, wg, gb, lam, wgl, wb, wo, ln_g, ln_b, l, *, seq, tm=512):
    t, d = x.shape
    return pl.pallas_call(
        functools.partial(_mix_out_kernel, seq // tm),
        grid=(t // tm,),
        in_specs=[
            pl.BlockSpec((tm, BR_W), lambda i: (i, 0)),
            pl.BlockSpec((tm, 2 * LRU_W), lambda i: (i, 0)),
            pl.BlockSpec((tm, BR_W), lambda i: (i, 0)),
            pl.BlockSpec((tm, d), lambda i: (i, 0)),
            pl.BlockSpec((tm, d), lambda i: (i, 0)),
            _pick(wg, l), _pick(gb, l), _pick(lam, l),
            _pick(wgl, l), _pick(wb, l), _pick(wo, l), _pick(ln_g, l, 1), _pick(ln_b, l, 1),
        ],
        out_specs=[pl.BlockSpec((tm, d), lambda i: (i, 0)), pl.BlockSpec((tm, d), lambda i: (i, 0))],
        out_shape=[jax.ShapeDtypeStruct((t, d), F32), jax.ShapeDtypeStruct((t, d), BF16)],
        scratch_shapes=[pltpu.VMEM((1, LRU_W), F32)],
        compiler_params=_params("arbitrary"),
        name="mix_out",
    )(ya, xg, yc, xb, x, wg, gb, lam, wgl, wb, wo, ln_g, ln_b)


def _xattn_kernel(xb_ref, x_ref, kv_ref, wq_ref, wo_ref, g_ref, b_ref, o_ref, ob_ref):
    subs = _sub_tiles(x_ref.shape[1])
    q = [(_dot(xb_ref[0, rs, :], wq_ref[...]) * np.float32(XA_HD ** -0.5)).astype(BF16) for rs in subs]

    def scores(task):
        s, h = task
        sl = slice(h * XA_HD, (h + 1) * XA_HD)
        return _dot_nt(q[s][:, sl], kv_ref[0, :, sl])

    tasks = [(s, h) for h in range(XA_HEADS) for s in range(len(subs))]
    outs = [[None] * XA_HEADS for _ in subs]
    s_next = scores(tasks[0])
    for ti, (s, h) in enumerate(tasks):
        v = kv_ref[0, :, D_MODEL + h * XA_HD:D_MODEL + (h + 1) * XA_HD]
        sc = s_next
        if ti + 1 < len(tasks):
            s_next = scores(tasks[ti + 1])
        p = jnp.exp(sc - sc.max(axis=-1, keepdims=True))
        den = p.sum(axis=-1, keepdims=True)
        outs[s][h] = (_dot(p.astype(BF16), v) / den).astype(BF16)
    y = [_dot(jnp.concatenate(o, axis=1), wo_ref[...]) for o in outs]
    for rs, yy in zip(subs, y):
        out = _layer_norm(ALPHA * x_ref[0, rs, :] + yy, g_ref[...], b_ref[...])
        o_ref[0, rs, :] = out
        ob_ref[0, rs, :] = out.astype(BF16)


def _xattn(xb, x, kv, wq, wo, ln_g, ln_b, l, *, tm=512):
    batch, seq, d = x.shape
    return pl.pallas_call(
        _xattn_kernel,
        grid=(batch, seq // tm),
        in_specs=[
            pl.BlockSpec((1, tm, d), lambda bi, i: (bi, i, 0)),
            pl.BlockSpec((1, tm, d), lambda bi, i: (bi, i, 0)),
            pl.BlockSpec((1, MEM_LEN, 2 * d), lambda bi, i: (bi, 0, 0)),
            _pick(wq, l), _pick(wo, l), _pick(ln_g, l, 2), _pick(ln_b, l, 2),
        ],
        out_specs=[
            pl.BlockSpec((1, tm, d), lambda bi, i: (bi, i, 0)),
            pl.BlockSpec((1, tm, d), lambda bi, i: (bi, i, 0)),
        ],
        out_shape=[jax.ShapeDtypeStruct((batch, seq, d), F32), jax.ShapeDtypeStruct((batch, seq, d), BF16)],
        compiler_params=_params("parallel", "parallel"),
        name="mem_xattn",
    )(xb, x, kv, wq, wo, ln_g, ln_b)


def _prepare(p):
    d = D_MODEL
    depth = p["ln_g"].shape[0]
    wt = jnp.swapaxes(p["mix_w_in"].astype(F32), 1, 2)
    hw = LA_HEADS * LA_HD
    w_packed = jnp.concatenate([
        wt[:, :_O_DB],
        jnp.pad(wt[:, _O_DB:_O_LX], ((0, 0), (0, LANES - 2 * DN_HEADS), (0, 0))),
        wt[:, _O_LX:_O_AQ],
        wt[:, _O_AQ:_O_AQ + hw] * np.float32(LA_HD ** -0.5),
        wt[:, _O_AQ + hw:_O_GL]], axis=1).astype(BF16)
    w_packed = jnp.swapaxes(w_packed, 1, 2)
    wgl = wt[:, _O_GL:].astype(BF16)

    def lane_row(v):
        return jnp.pad(v.astype(F32), ((0, 0), (DN_HEADS, LANES - 2 * DN_HEADS)))[:, None, :]

    eye = jnp.eye(LRU_BLOCKS, dtype=F32)
    wg = jnp.einsum("lgnde,nm->lgndme", p["lru_gate_w"].astype(F32), eye)
    wg = wg.reshape(depth, 2, LRU_W, LRU_W)
    wg = jnp.concatenate([wg[:, 0], wg[:, 1]], axis=2).astype(BF16)
    return dict(
        ln_g=p["ln_g"].astype(F32).reshape(depth, 4, 1, d),
        ln_b=p["ln_b"].astype(F32).reshape(depth, 4, 1, d),
        w12=p["ffn_w12"].astype(BF16), w3=p["ffn_w3"].astype(BF16),
        w_in=w_packed, wgl=wgl,
        dcw=p["dn_conv_w"].astype(F32), lcw=p["lru_conv_w"].astype(F32),
        lcb=p["lru_conv_b"].astype(F32)[:, None, :],
        alog=lane_row(p["dn_a_log"]), dtb=lane_row(p["dn_dt_bias"]),
        nw=p["dn_norm_w"].astype(F32)[:, None, :],
        wg=wg, gb=p["lru_gate_b"].astype(F32).reshape(depth, 1, 2 * LRU_W),
        lam=p["lru_lambda"].astype(F32)[:, None, :],
        rel=_band_rel_rows(p["la_rel_bias"]),
        wb=p["w_branch"].astype(BF16), wout=p["mix_w_out"].astype(BF16),
        wq=p["xa_wq"].astype(BF16), wkv=p["xa_wkv"].astype(BF16), wo=p["xa_wo"].astype(BF16))


def _layer(x, mem_b, l, w):
    batch, seq, d = x.shape
    t = batch * seq

    x, xb = _ffn(x.reshape(t, d), w["w12"], w["w3"], w["ln_g"], w["ln_b"], (l, 0))

    qkvz, ba, xg, aqkv = _mixer_in(xb, w["w_in"], w["dcw"], w["lcw"], w["lcb"], l, seq=seq)
    qkvz = qkvz.reshape(batch, seq, -1)
    ya = _delta(qkvz, ba.reshape(batch, seq, LANES), w["alog"], w["dtb"], w["nw"], l)
    yc = _band_attention(aqkv.reshape(batch, seq, -1), w["rel"], l, batch=batch, seq=seq)
    x, xb = _mix_out(ya.reshape(t, -1), xg, yc.reshape(t, -1), xb, x, w["wg"], w["gb"], w["lam"],
                     w["wgl"], w["wb"], w["wout"], w["ln_g"], w["ln_b"], l, seq=seq)

    kv = _proj_plain(mem_b, w["wkv"], l, tn=D_MODEL, out_dtype=BF16, name="proj_kv")
    x, xb = _xattn(xb.reshape(batch, seq, d), x.reshape(batch, seq, d), kv.reshape(batch, MEM_LEN, 2 * d),
                   w["wq"], w["wo"], w["ln_g"], w["ln_b"], l)

    x, _ = _ffn(x.reshape(t, d), w["w12"], w["w3"], w["ln_g"], w["ln_b"], (l, 1))
    return x.reshape(batch, seq, d)


def kernel(x, mem, ln_g, ln_b, ffn_w12, ffn_w3, mix_w_in, dn_conv_w, dn_a_log, dn_dt_bias, dn_norm_w,
           lru_conv_w, lru_conv_b, lru_gate_w, lru_gate_b, lru_lambda, la_rel_bias, w_branch, mix_w_out,
           xa_wq, xa_wkv, xa_wo):
    w = _prepare(dict(
        ln_g=ln_g, ln_b=ln_b, ffn_w12=ffn_w12, ffn_w3=ffn_w3, mix_w_in=mix_w_in, dn_conv_w=dn_conv_w,
        dn_a_log=dn_a_log, dn_dt_bias=dn_dt_bias, dn_norm_w=dn_norm_w, lru_conv_w=lru_conv_w,
        lru_conv_b=lru_conv_b, lru_gate_w=lru_gate_w, lru_gate_b=lru_gate_b, lru_lambda=lru_lambda,
        la_rel_bias=la_rel_bias, w_branch=w_branch, mix_w_out=mix_w_out, xa_wq=xa_wq, xa_wkv=xa_wkv,
        xa_wo=xa_wo))
    batch = x.shape[0]
    mem_b = mem.reshape(batch * MEM_LEN, D_MODEL).astype(BF16)
    x = x.astype(F32)
    for l in range(DEPTH):
        x = _layer(x, mem_b, l, w)
    return x
```

```python
import functools

import numpy as np
import jax
import jax.numpy as jnp
from jax import lax
from jax.experimental import pallas as pl
from jax.experimental.pallas import tpu as pltpu

F32 = jnp.float32
BF16 = jnp.bfloat16

D_MODEL = 1024
DEPTH = 2
CHUNK = 64
CONV_W = 4
BR_W = 512
N_BRANCH = 3
DN_HEADS = 4
DN_DK = 128
DN_DV = 128
LRU_W = BR_W
LRU_BLOCKS = 8
LRU_C = 8.0
LA_HEADS = 8
LA_HD = 64
LA_PAST = 8
REL_CLIP = 128
MEM_LEN = 256
XA_HEADS = 4
XA_HD = D_MODEL // XA_HEADS
D_FF = 2816
ALPHA = (2 * DEPTH) ** 0.25
LN_EPS = 1e-5
NORM_EPS = 1e-6
NEG_INF = -1e30

LANES = 128
SUBLANES = 8
VMEM_LIMIT = 56 * 1024 * 1024

_O_DZ = 3 * DN_HEADS * DN_DK
_O_DB = _O_DZ + DN_HEADS * DN_DV
_O_LX = _O_DB + 2 * DN_HEADS
_O_AQ = _O_LX + 2 * LRU_W
_O_GL = _O_AQ + 3 * LA_HEADS * LA_HD


def _params(*sem):
    return pltpu.CompilerParams(dimension_semantics=sem, vmem_limit_bytes=VMEM_LIMIT)


def _dot(a, b):
    return jnp.dot(a, b, preferred_element_type=F32)


def _dot_nt(a, b):
    return lax.dot_general(a, b, (((1,), (1,)), ((), ())), preferred_element_type=F32)


def _sigmoid(x):
    return 1.0 / (1.0 + jnp.exp(-x))


def _softplus(x):
    return jnp.maximum(x, 0.0) + jnp.log1p(jnp.exp(-jnp.abs(x)))


def _gelu_tanh(x):
    c = np.float32(np.sqrt(2.0 / np.pi))
    return 0.5 * x * (1.0 + jnp.tanh(c * (x + np.float32(0.044715) * (x * x * x))))


def _layer_norm(y, g, b):
    mu = jnp.mean(y, axis=-1, keepdims=True)
    d = y - mu
    var = jnp.mean(d * d, axis=-1, keepdims=True)
    return d * lax.rsqrt(var + LN_EPS) * g + b


def _pick(arr, *idx):
    shape = (None,) * len(idx) + tuple(arr.shape[len(idx):])
    index = tuple(idx) + (0,) * (arr.ndim - len(idx))
    return pl.BlockSpec(shape, lambda *_: index, pipeline_mode=pl.Buffered(1))


_FF_CHUNK = 256


def _ffn_kernel(x_ref, w12_ref, w3_ref, g_ref, b_ref, o_ref, ob_ref):
    x = x_ref[...]
    xb = x.astype(BF16)
    n = D_FF // _FF_CHUNK

    def gate_up(c):
        lo = c * _FF_CHUNK
        return (_dot(xb, w12_ref[:, lo:lo + _FF_CHUNK]),
                _dot(xb, w12_ref[:, D_FF + lo:D_FF + lo + _FF_CHUNK]))

    nxt = gate_up(0)
    acc = None
    for c in range(n):
        g, u = nxt
        if c + 1 < n:
            nxt = gate_up(c + 1)
        h = (g * _sigmoid(g) * u).astype(BF16)
        part = _dot(h, w3_ref[c * _FF_CHUNK:(c + 1) * _FF_CHUNK, :])
        acc = part if acc is None else acc + part
    out = _layer_norm(ALPHA * x + 0.5 * acc, g_ref[...], b_ref[...])
    o_ref[...] = out
    ob_ref[...] = out.astype(BF16)


def _ffn(x, w12, w3, ln_g, ln_b, sel, *, tm=1024):
    t, d = x.shape
    l, k = sel
    return pl.pallas_call(
        _ffn_kernel,
        grid=(t // tm,),
        in_specs=[
            pl.BlockSpec((tm, d), lambda i: (i, 0)),
            _pick(w12, l, k),
            _pick(w3, l, k),
            _pick(ln_g, l, 3 * k),
            _pick(ln_b, l, 3 * k),
        ],
        out_specs=[
            pl.BlockSpec((tm, d), lambda i: (i, 0)),
            pl.BlockSpec((tm, d), lambda i: (i, 0)),
        ],
        out_shape=[jax.ShapeDtypeStruct((t, d), F32), jax.ShapeDtypeStruct((t, d), BF16)],
        compiler_params=_params("parallel"),
        name="ffn_ln",
    )(x, w12, w3, ln_g, ln_b)


_HW = DN_HEADS * DN_DK
_C_QKV = 0
_C_Z = 3 * _HW
_C_BA = _C_Z + _HW
_C_LX = _C_BA + LANES
_C_LG = _C_LX + LRU_W
_C_ATT = _C_LG + LRU_W
_SEG = 256
_N_TAILS = (3 * _HW + LRU_W) // _SEG


def _causal_conv(y, tail, cw):
    row8 = lax.broadcasted_iota(jnp.int32, (SUBLANES, y.shape[1]), 0)
    acc = y * cw[CONV_W - 1:CONV_W, :]
    fix = jnp.zeros((SUBLANES, y.shape[1]), F32)
    for k in range(1, CONV_W):
        wk = cw[CONV_W - 1 - k:CONV_W - k, :]
        sh = pltpu.roll(y, k, axis=0)
        acc = acc + sh * wk
        prev = pltpu.roll(tail, k, axis=0)
        fix = fix + jnp.where(row8 < k, (prev - sh[:SUBLANES]) * wk, 0.0)
    return jnp.concatenate([acc[:SUBLANES] + fix, acc[SUBLANES:]], axis=0)


def _mixer_in_kernel(tiles_per_seq, x_ref, w_ref, dcw_ref, lcw_ref, lcb_ref,
                     qkvz_ref, ba_ref, xg_ref, att_ref, tail_ref):
    xb = x_ref[...]
    first = (pl.program_id(0) % tiles_per_seq) == 0
    rows = xb.shape[0]

    def proj(c0, width=_SEG):
        return _dot(xb, w_ref[:, c0:c0 + width])

    def cols(j):
        return slice(j * _SEG, (j + 1) * _SEG)

    def conv(y, slot, cw):
        tail = jnp.where(first, 0.0, tail_ref[slot])
        tail_ref[slot] = y[rows - SUBLANES:]
        return _causal_conv(y, tail, cw)

    def delta_qkv(j, y):
        c = conv(y, j, dcw_ref[:, cols(j)])
        c = c * _sigmoid(c)
        if j * _SEG < 2 * _HW:
            scale = np.float32(DN_DK ** -0.5 if j * _SEG < _HW else 1.0)
            parts = []
            for h in range(_SEG // DN_DK):
                ch = c[:, h * DN_DK:(h + 1) * DN_DK]
                parts.append(ch * (lax.rsqrt(jnp.sum(ch * ch, axis=-1, keepdims=True) + NORM_EPS) * scale))
            c = jnp.concatenate(parts, axis=1)
        qkvz_ref[:, cols(j)] = c.astype(BF16)

    def z_part(j):
        qkvz_ref[:, 3 * _HW + j * _SEG:3 * _HW + (j + 1) * _SEG] = proj(_C_Z + j * _SEG).astype(BF16)

    def lru_x(j, y):
        slot = 3 * _HW // _SEG + j
        xg_ref[:, cols(j)] = (conv(y, slot, lcw_ref[:, cols(j)]) + lcb_ref[:, cols(j)]).astype(BF16)

    def lru_g(j, y):
        xg_ref[:, LRU_W + j * _SEG:LRU_W + (j + 1) * _SEG] = _gelu_tanh(y).astype(BF16)

    def att(j):
        att_ref[:, cols(j)] = proj(_C_ATT + j * _SEG).astype(BF16)

    heavy = ([(_C_QKV + j * _SEG, functools.partial(delta_qkv, j)) for j in range(3 * _HW // _SEG)]
             + [(_C_LX + j * _SEG, functools.partial(lru_x, j)) for j in range(LRU_W // _SEG)]
             + [(_C_LG + j * _SEG, functools.partial(lru_g, j)) for j in range(LRU_W // _SEG)])
    light = ([functools.partial(att, j) for j in range(3 * _HW // _SEG)]
             + [functools.partial(z_part, j) for j in range(_HW // _SEG)])
    for i, (c0, epilogue) in enumerate(heavy):
        epilogue(proj(c0))
        if i < len(light):
            light[i]()
    for g in light[len(heavy):]:
        g()
    ba_ref[...] = proj(_C_BA, LANES)


def _mixer_in(xb, w, dcw, lcw, lcb, l, *, seq, tm=512):
    t, d = xb.shape

    def tile(width):
        return pl.BlockSpec((tm, width), lambda i: (i, 0))

    widths = (4 * _HW, LANES, 2 * LRU_W, 3 * _HW)
    dtypes = (BF16, F32, BF16, BF16)
    return pl.pallas_call(
        functools.partial(_mixer_in_kernel, seq // tm),
        grid=(t // tm,),
        in_specs=[tile(d), _pick(w, l), _pick(dcw, l), _pick(lcw, l), _pick(lcb, l)],
        out_specs=[tile(wd) for wd in widths],
        out_shape=[jax.ShapeDtypeStruct((t, wd), dt) for wd, dt in zip(widths, dtypes)],
        scratch_shapes=[pltpu.VMEM((_N_TAILS, SUBLANES, _SEG), F32)],
        compiler_params=_params("arbitrary"),
        name="mixer_in",
    )(xb, w, dcw, lcw, lcb)


def _proj_plain_kernel(x_ref, w_ref, o_ref):
    o_ref[...] = _dot(x_ref[...], w_ref[...]).astype(o_ref.dtype)


def _proj_plain(xb, w, l, *, tn, out_dtype, name):
    t, d = xb.shape
    n = w.shape[2]
    return pl.pallas_call(
        _proj_plain_kernel,
        grid=(n // tn,),
        in_specs=[pl.BlockSpec((t, d), lambda j: (0, 0)), pl.BlockSpec((None, d, tn), lambda j: (l, 0, j))],
        out_specs=pl.BlockSpec((t, tn), lambda j: (0, j)),
        out_shape=jax.ShapeDtypeStruct((t, n), out_dtype),
        compiler_params=_params("parallel"),
        name=name,
    )(xb, w)


_DELTA_CHUNKS = 4
_PREP_YIELD = 8
_SCAN_HEAD = 6
_SCAN_EVERY = 8
_FACTOR_SHAPES = ((DN_HEADS * 2 * CHUNK, DN_DK, BF16),
                  (DN_HEADS * CHUNK, DN_DV, F32),
                  (DN_HEADS * CHUNK, CHUNK, BF16),
                  (DN_HEADS * DN_DK, CHUNK, BF16),
                  (DN_HEADS * SUBLANES, LANES, F32))


def _segmented_cumsum_rows(x, seg):
    row = lax.broadcasted_iota(jnp.int32, x.shape, 0) & (seg - 1)
    d = 1
    while d < seg:
        x = x + jnp.where(row >= d, pltpu.roll(x, d, axis=0), 0.0)
        d *= 2
    return x


def _delta_prep_stages(batch, slot, q_ref, k_ref, v_ref, ba_ref, alog_ref, dtb_ref, factors):
    wq_s, u_s, qk_s, kdt_s, gl_s = factors
    row = lax.broadcasted_iota(jnp.int32, (CHUNK, CHUNK), 0)
    col = lax.broadcasted_iota(jnp.int32, (CHUNK, CHUNK), 1)
    lower = row >= col
    strict = row > col
    eye = jnp.where(row == col, 1.0, 0.0).astype(F32)
    zpad = jnp.zeros((LANES - CHUNK, LANES), F32)

    beta_all, g_cum, g_cum_t = [], [], []
    for b in range(batch):
        ba = ba_ref[b]
        beta_all.append(_sigmoid(ba))
        g = _segmented_cumsum_rows(-jnp.exp(alog_ref[...]) * _softplus(ba + dtb_ref[...]), CHUNK)
        g_cum.append(g)
        g_cum_t.append([jnp.transpose(g[pr * LANES:(pr + 1) * LANES]) for pr in range(_DELTA_CHUNKS // 2)])

    inst = [(b, ci, h) for b in range(batch) for ci in range(_DELTA_CHUNKS) for h in range(DN_HEADS)]

    def paced(items):
        for n, item in enumerate(items):
            yield item
            if n % _PREP_YIELD == _PREP_YIELD - 1:
                yield None

    st = []
    for it in paced(inst):
        if it is None:
            yield
            continue
        b, ci, h = it
        rs = slice(ci * CHUNK, (ci + 1) * CHUNK)
        hs = slice(h * DN_DK, (h + 1) * DN_DK)
        q = q_ref[b, rs, hs].astype(F32)
        k = k_ref[b, rs, hs].astype(F32)
        beta = beta_all[b][rs, h:h + 1]
        gc = g_cum[b][rs, DN_HEADS + h:DN_HEADS + h + 1]
        gr = g_cum_t[b][ci // 2][DN_HEADS + h:DN_HEADS + h + 1, (ci % 2) * CHUNK:(ci % 2 + 1) * CHUNK]
        decay = jnp.exp(jnp.where(lower, gc - gr, NEG_INF))
        kb = k * beta
        aq = _dot_nt(jnp.concatenate([kb, q], axis=0).astype(BF16), k.astype(BF16))
        st.append(dict(b=b, ci=ci, h=h, rs=rs, hs=hs, beta=beta, gc=gc, kb=kb, aq=aq, decay=decay))

    for s in st:
        s["p"] = -jnp.where(strict, s["aq"][:CHUNK] * s["decay"], 0.0)
        s["tinv"] = eye + s["p"]
    n = 1
    while 2 * n < CHUNK:
        for s in paced(st):
            if s is None:
                yield
                continue
            p16 = s["p"].astype(BF16)
            s["p"] = _dot(p16, p16)
        for s in paced(st):
            if s is None:
                yield
                continue
            s["tinv"] = s["tinv"] + _dot(s["tinv"].astype(BF16), s["p"].astype(BF16))
        n *= 2

    for s in paced(st):
        if s is None:
            yield
            continue
        exp_g = jnp.exp(s["gc"])
        v = v_ref[s["b"], s["rs"], s["hs"]].astype(F32)
        rhs = jnp.concatenate([v * s["beta"], s["kb"] * exp_g], axis=1)
        s["sol"] = _dot(s["tinv"].astype(BF16), rhs.astype(BF16))
        s["exp_g"] = exp_g

    for s in paced(st):
        if s is None:
            yield
            continue
        b, ci, h = s["b"], s["ci"], s["h"]
        c = b * _DELTA_CHUNKS + ci
        q = q_ref[b, s["rs"], s["hs"]].astype(F32)
        k = k_ref[b, s["rs"], s["hs"]].astype(F32)
        g_last = s["gc"][CHUNK - 1:CHUNK, :]
        k_dec = k * jnp.exp(g_last - s["gc"])
        kdt = jnp.transpose(jnp.concatenate([k_dec, zpad], axis=0))[:, :CHUNK]
        wq_s[slot, c, h * 2 * CHUNK:(h + 1) * 2 * CHUNK, :] = jnp.concatenate(
            [s["sol"][:, DN_DV:], q * s["exp_g"]], axis=0).astype(BF16)
        u_s[slot, c, h * CHUNK:(h + 1) * CHUNK, :] = s["sol"][:, :DN_DV]
        qk_s[slot, c, h * CHUNK:(h + 1) * CHUNK, :] = (s["aq"][CHUNK:] * s["decay"]).astype(BF16)
        kdt_s[slot, c, h * DN_DK:(h + 1) * DN_DK, :] = kdt.astype(BF16)
        gl_s[slot, c, h * SUBLANES:(h + 1) * SUBLANES, :] = jnp.broadcast_to(
            jnp.exp(g_last), (SUBLANES, LANES))


def _delta_scan_stages(batch, slot, factors, z_ref, nw_ref, o_ref, state_ref):
    wq_s, u_s, qk_s, kdt_s, gl_s = factors
    chains = [(b, h) for b in range(batch) for h in range(DN_HEADS)]
    for cc in range(_DELTA_CHUNKS):
        rs = slice(cc * CHUNK, (cc + 1) * CHUNK)
        r = [_dot(wq_s[slot, b * _DELTA_CHUNKS + cc, h * 2 * CHUNK:(h + 1) * 2 * CHUNK, :],
                  state_ref[b * DN_HEADS + h].astype(BF16)) for b, h in chains]
        yield
        v_new = [(u_s[slot, b * _DELTA_CHUNKS + cc, h * CHUNK:(h + 1) * CHUNK, :] - r[i][:CHUNK]).astype(BF16)
                 for i, (b, h) in enumerate(chains)]
        kv = [_dot(kdt_s[slot, b * _DELTA_CHUNKS + cc, h * DN_DK:(h + 1) * DN_DK, :], v_new[i])
              for i, (b, h) in enumerate(chains)]
        qv = [_dot(qk_s[slot, b * _DELTA_CHUNKS + cc, h * CHUNK:(h + 1) * CHUNK, :], v_new[i])
              for i, (b, h) in enumerate(chains)]
        yield
        for i, (b, h) in enumerate(chains):
            state_ref[b * DN_HEADS + h] = (
                state_ref[b * DN_HEADS + h]
                * gl_s[slot, b * _DELTA_CHUNKS + cc, h * SUBLANES:h * SUBLANES + 1, :] + kv[i])
            o = r[i][CHUNK:] + qv[i]
            z = z_ref[b, rs, h * DN_DV:(h + 1) * DN_DV].astype(F32)
            o = o * lax.rsqrt(jnp.mean(o * o, axis=-1, keepdims=True) + NORM_EPS)
            o = o * nw_ref[...] * (z * _sigmoid(z))
            o_ref[b, rs, h * DN_DV:(h + 1) * DN_DV] = o.astype(o_ref.dtype)


def _delta_kernel(batch, q_ref, k_ref, v_ref, ba_ref, z_ref, alog_ref, dtb_ref, nw_ref, o_ref,
                  wq_s, u_s, qk_s, kdt_s, gl_s, state_ref):
    i = pl.program_id(0)
    factors = (wq_s, u_s, qk_s, kdt_s, gl_s)

    @pl.when(i == 0)
    def _():
        state_ref[...] = jnp.zeros_like(state_ref)
        for f in factors:
            f[...] = jnp.zeros_like(f)

    cur = i % 2
    prep = _delta_prep_stages(batch, cur, q_ref, k_ref, v_ref, ba_ref, alog_ref, dtb_ref, factors)
    scan = _delta_scan_stages(batch, 1 - cur, factors, z_ref, nw_ref, o_ref, state_ref)
    for _ in range(_SCAN_HEAD):
        next(scan, None)
    for n, _ in enumerate(prep):
        if n % _SCAN_EVERY == 0:
            next(scan, None)
    for _ in scan:
        pass


def _delta(qkvz, ba, alog, dtb, nw, l):
    batch, seq, _ = qkvz.shape
    rows = _DELTA_CHUNKS * CHUNK
    nb = seq // rows
    hw = DN_HEADS * DN_DK

    def cur_spec(width, cb):
        return pl.BlockSpec((batch, rows, width), lambda i: (0, jnp.minimum(i, nb - 1), cb))

    def prev_spec(cb):
        return pl.BlockSpec((batch, rows, hw), lambda i: (0, jnp.maximum(i - 1, 0), cb))

    return pl.pallas_call(
        functools.partial(_delta_kernel, batch),
        grid=(nb + 1,),
        in_specs=[cur_spec(hw, 0), cur_spec(hw, 1), cur_spec(hw, 2), cur_spec(LANES, 0), prev_spec(3),
                  _pick(alog, l), _pick(dtb, l), _pick(nw, l)],
        out_specs=prev_spec(0),
        out_shape=jax.ShapeDtypeStruct((batch, seq, hw), BF16),
        scratch_shapes=[pltpu.VMEM((2, batch * _DELTA_CHUNKS, r, c), dt) for r, c, dt in _FACTOR_SHAPES]
        + [pltpu.VMEM((batch * DN_HEADS, DN_DK, DN_DV), F32)],
        compiler_params=_params("arbitrary"),
        name="gated_delta",
    )(qkvz, qkvz, qkvz, ba, qkvz, alog, dtb, nw)


def _lru_stages(xg, gates, lam, state, out):
    rows = xg.shape[0]
    xc = xg[:, :LRU_W].astype(F32)
    r = _sigmoid(gates[:, :LRU_W])
    ig = _sigmoid(gates[:, LRU_W:])
    yield
    log_a = -LRU_C * r * _softplus(-lam)
    a = jnp.exp(log_a)
    u = xc * ig * jnp.sqrt(-jnp.tanh(log_a) * (1.0 + a * a))
    yield
    row = lax.broadcasted_iota(jnp.int32, a.shape, 0)
    d = 1
    while d < rows:
        a_sh = jnp.where(row >= d, pltpu.roll(a, d, axis=0), 1.0)
        u_sh = jnp.where(row >= d, pltpu.roll(u, d, axis=0), 0.0)
        u = a * u_sh + u
        a = a * a_sh
        d *= 2
        yield
    h = u + a * state[0]
    state[0] = h[rows - 1:rows, :]
    out.append((h * xg[:, LRU_W:].astype(F32)).astype(BF16))


_QB = 4 * CHUNK
_KB = 3
_CHUNK_SHIFT = CHUNK.bit_length() - 1


def _band_kernel(q_ref, k0_ref, k1_ref, k2_ref, v0_ref, v1_ref, v2_ref, rel_ref, o_ref, bias_ref):
    i = pl.program_id(1)

    @pl.when((pl.program_id(0) == 0) & (i == 0))
    def _():
        qpos = lax.broadcasted_iota(jnp.int32, (_QB, _QB), 0)
        kcol = lax.broadcasted_iota(jnp.int32, (_QB, _QB), 1)
        for m in range(_KB):
            chunk_off = ((kcol + (m - (_KB - 1)) * _QB) >> _CHUNK_SHIFT) - (qpos >> _CHUNK_SHIFT)
            valid = (chunk_off <= 0) & (chunk_off >= -LA_PAST)
            for h in range(LA_HEADS):
                ev = jnp.broadcast_to(rel_ref[h, m:m + 1, :], (_QB, 2 * _QB))
                toeplitz = pltpu.roll(ev, _QB, axis=1, stride=1, stride_axis=0)[:, :_QB]
                bias_ref[h, m] = jnp.where(valid, toeplitz, NEG_INF)

    k_refs = (k0_ref, k1_ref, k2_ref)
    v_refs = (v0_ref, v1_ref, v2_ref)
    lane = lax.broadcasted_iota(jnp.int32, (1, LANES), 1)
    half_sel = (lane < LA_HD, lane >= LA_HD)

    def attend(ms):
        def scores(h):
            ps = slice((h // 2) * LANES, (h // 2 + 1) * LANES)
            qh = jnp.where(half_sel[h % 2], q_ref[0, :, ps], jnp.zeros((), BF16))
            return [_dot_nt(qh, k_refs[m][0, :, ps]) for m in ms]

        s_next = scores(0)
        pair_out = None
        for h in range(LA_HEADS):
            ps = slice((h // 2) * LANES, (h // 2 + 1) * LANES)
            sel = half_sel[h % 2]
            s = s_next
            if h + 1 < LA_HEADS:
                s_next = scores(h + 1)
            s = [sm + bias_ref[h, m] for sm, m in zip(s, ms)]
            mx = s[0].max(axis=-1, keepdims=True)
            for sm in s[1:]:
                mx = jnp.maximum(mx, sm.max(axis=-1, keepdims=True))
            acc = None
            for sm, m in zip(s, ms):
                vh = jnp.where(sel, v_refs[m][0, :, ps], jnp.ones((), BF16))
                part = _dot(jnp.exp((sm - mx).astype(BF16)), vh)
                acc = part if acc is None else acc + part
            den = pltpu.roll(acc, LA_HD, axis=1)
            out = jnp.where(sel, acc / den, 0.0)
            if h % 2 == 0:
                pair_out = out
            else:
                o_ref[0, :, ps] = (pair_out + out).astype(o_ref.dtype)

    for nvalid in range(1, _KB + 1):
        pl.when(jnp.minimum(i, _KB - 1) == nvalid - 1)(
            functools.partial(attend, tuple(range(_KB - nvalid, _KB))))


def _band_attention(qkv, rel_rows, l, *, batch, seq):
    hw = LA_HEADS * LA_HD
    nblk = seq // _QB

    def kv_spec(col, m):
        return pl.BlockSpec((1, _QB, hw), lambda b, i: (b, jnp.maximum(i - (_KB - 1 - m), 0), col))

    return pl.pallas_call(
        _band_kernel,
        grid=(batch, nblk),
        in_specs=[pl.BlockSpec((1, _QB, hw), lambda b, i: (b, i, 0))]
        + [kv_spec(1, m) for m in range(_KB)]
        + [kv_spec(2, m) for m in range(_KB)]
        + [_pick(rel_rows, l)],
        out_specs=pl.BlockSpec((1, _QB, hw), lambda b, i: (b, i, 0)),
        out_shape=jax.ShapeDtypeStruct((batch, seq, hw), BF16),
        scratch_shapes=[pltpu.VMEM((LA_HEADS, _KB, _QB, _QB), F32)],
        compiler_params=_params("arbitrary", "arbitrary"),
        name="band_attention",
    )(qkv, qkv, qkv, qkv, qkv, qkv, qkv, rel_rows)


def _band_rel_rows(rel_table):
    t = rel_table.astype(F32)
    lo = _KB * _QB - REL_CLIP
    hi = _QB - 1 - REL_CLIP
    full = jnp.concatenate([jnp.broadcast_to(t[..., :1], t.shape[:-1] + (lo,)), t,
                            jnp.broadcast_to(t[..., -1:], t.shape[:-1] + (hi,))], axis=-1)
    return jnp.stack([full[..., m * _QB:(m + 2) * _QB] for m in range(_KB)], axis=-2)


_SUB_TILES = 2
_LRU_STAGES_PER_DOT = 3


def _sub_tiles(rows):
    step = rows // _SUB_TILES
    return [slice(s * step, (s + 1) * step) for s in range(_SUB_TILES)]


def _mix_out_kernel(tiles_per_seq, ya_ref, xg_ref, yc_ref, xb_ref, x_ref, wg_ref, gb_ref, lam_ref,
                    wgl_ref, wb_ref, wo_ref, g_ref, b_ref, o_ref, ob_ref, h_ref):
    subs = _sub_tiles(x_ref.shape[0])
    lru_pre = [_dot(xg_ref[rs, :LRU_W], wg_ref[...]) for rs in subs]
    first = (pl.program_id(0) % tiles_per_seq) == 0
    state = [jnp.where(first, 0.0, h_ref[...])]
    yb = []

    def lru_all():
        for rs, pre in zip(subs, lru_pre):
            yield from _lru_stages(xg_ref[rs, :], pre + gb_ref[...], lam_ref[...], state, yb)

    lru = lru_all()
    up_a, up_c, gl = [], [], [[] for _ in subs]
    free_dots = ([lambda s=s, rs=rs, r=r: gl[s].append(
                     _dot_nt(xb_ref[rs, :], wgl_ref[r * D_MODEL:(r + 1) * D_MODEL, :]))
                    for s, rs in enumerate(subs) for r in range(N_BRANCH)]
                 + [lambda rs=rs: up_a.append(_dot(ya_ref[rs, :], wb_ref[0])) for rs in subs]
                 + [lambda rs=rs: up_c.append(_dot(yc_ref[rs, :], wb_ref[2])) for rs in subs])
    for issue in free_dots:
        issue()
        for _ in range(_LRU_STAGES_PER_DOT):
            next(lru, None)
    for _ in lru:
        pass
    h_ref[...] = state[0]
    up_b = [_dot(y, wb_ref[1]) for y in yb]
    merged = []
    for s in range(len(subs)):
        m = (_sigmoid(gl[s][0]) * up_a[s] + _sigmoid(gl[s][1]) * up_b[s]) + _sigmoid(gl[s][2]) * up_c[s]
        merged.append(m.astype(BF16))
    y = [_dot(m, wo_ref[...]) for m in merged]
    for rs, yy in zip(subs, y):
        out = _layer_norm(ALPHA * x_ref[rs, :] + yy, g_ref[...], b_ref[...])
        o_ref[rs, :] = out
        ob_ref[rs, :] = out.astype(BF16)


def _mix_out(ya, xg, yc, xb, x, wg, gb, lam, wgl, wb, wo, ln_g, ln_b, l, *, seq, tm=512):
    t, d = x.shape
    return pl.pallas_call(
        functools.partial(_mix_out_kernel, seq // tm),
        grid=(t // tm,),
        in_specs=[
            pl.BlockSpec((tm, BR_W), lambda i: (i, 0)),
            pl.BlockSpec((tm, 2 * LRU_W), lambda i: (i, 0)),
            pl.BlockSpec((tm, BR_W), lambda i: (i, 0)),
            pl.BlockSpec((tm, d), lambda i: (i, 0)),
            pl.BlockSpec((tm, d), lambda i: (i, 0)),
            _pick(wg, l), _pick(gb, l), _pick(lam, l),
            _pick(wgl, l), _pick(wb, l), _pick(wo, l), _pick(ln_g, l, 1), _pick(ln_b, l, 1),
        ],
        out_specs=[pl.BlockSpec((tm, d), lambda i: (i, 0)), pl.BlockSpec((tm, d), lambda i: (i, 0))],
        out_shape=[jax.ShapeDtypeStruct((t, d), F32), jax.ShapeDtypeStruct((t, d), BF16)],
        scratch_shapes=[pltpu.VMEM((1, LRU_W), F32)],
        compiler_params=_params("arbitrary"),
        name="mix_out",
    )(ya, xg, yc, xb, x, wg, gb, lam, wgl, wb, wo, ln_g, ln_b)


def _xattn_kernel(xb_ref, x_ref, kv_ref, wq_ref, wo_ref, g_ref, b_ref, o_ref, ob_ref):
    subs = _sub_tiles(x_ref.shape[1])
    q = [(_dot(xb_ref[0, rs, :], wq_ref[...]) * np.float32(XA_HD ** -0.5)).astype(BF16) for rs in subs]

    def scores(task):
        s, h = task
        sl = slice(h * XA_HD, (h + 1) * XA_HD)
        return _dot_nt(q[s][:, sl], kv_ref[0, :, sl])

    tasks = [(s, h) for h in range(XA_HEADS) for s in range(len(subs))]
    outs = [[None] * XA_HEADS for _ in subs]
    s_next = scores(tasks[0])
    for ti, (s, h) in enumerate(tasks):
        v = kv_ref[0, :, D_MODEL + h * XA_HD:D_MODEL + (h + 1) * XA_HD]
        sc = s_next
        if ti + 1 < len(tasks):
            s_next = scores(tasks[ti + 1])
        p = jnp.exp(sc - sc.max(axis=-1, keepdims=True))
        den = p.sum(axis=-1, keepdims=True)
        outs[s][h] = (_dot(p.astype(BF16), v) / den).astype(BF16)
    y = [_dot(jnp.concatenate(o, axis=1), wo_ref[...]) for o in outs]
    for rs, yy in zip(subs, y):
        out = _layer_norm(ALPHA * x_ref[0, rs, :] + yy, g_ref[...], b_ref[...])
        o_ref[0, rs, :] = out
        ob_ref[0, rs, :] = out.astype(BF16)


def _xattn(xb, x, kv, wq, wo, ln_g, ln_b, l, *, tm=1024):
    batch, seq, d = x.shape
    return pl.pallas_call(
        _xattn_kernel,
        grid=(batch, seq // tm),
        in_specs=[
            pl.BlockSpec((1, tm, d), lambda bi, i: (bi, i, 0)),
            pl.BlockSpec((1, tm, d), lambda bi, i: (bi, i, 0)),
            pl.BlockSpec((1, MEM_LEN, 2 * d), lambda bi, i: (bi, 0, 0)),
            _pick(wq, l), _pick(wo, l), _pick(ln_g, l, 2), _pick(ln_b, l, 2),
        ],
        out_specs=[
            pl.BlockSpec((1, tm, d), lambda bi, i: (bi, i, 0)),
            pl.BlockSpec((1, tm, d), lambda bi, i: (bi, i, 0)),
        ],
        out_shape=[jax.ShapeDtypeStruct((batch, seq, d), F32), jax.ShapeDtypeStruct((batch, seq, d), BF16)],
        compiler_params=_params("parallel", "parallel"),
        name="mem_xattn",
    )(xb, x, kv, wq, wo, ln_g, ln_b)


def _prepare(p):
    d = D_MODEL
    depth = p["ln_g"].shape[0]
    wt = jnp.swapaxes(p["mix_w_in"].astype(F32), 1, 2)
    hw = LA_HEADS * LA_HD
    w_packed = jnp.concatenate([
        wt[:, :_O_DB],
        jnp.pad(wt[:, _O_DB:_O_LX], ((0, 0), (0, LANES - 2 * DN_HEADS), (0, 0))),
        wt[:, _O_LX:_O_AQ],
        wt[:, _O_AQ:_O_AQ + hw] * np.float32(LA_HD ** -0.5),
        wt[:, _O_AQ + hw:_O_GL]], axis=1).astype(BF16)
    w_packed = jnp.swapaxes(w_packed, 1, 2)
    wgl = wt[:, _O_GL:].astype(BF16)

    def lane_row(v):
        return jnp.pad(v.astype(F32), ((0, 0), (DN_HEADS, LANES - 2 * DN_HEADS)))[:, None, :]

    eye = jnp.eye(LRU_BLOCKS, dtype=F32)
    wg = jnp.einsum("lgnde,nm->lgndme", p["lru_gate_w"].astype(F32), eye)
    wg = wg.reshape(depth, 2, LRU_W, LRU_W)
    wg = jnp.concatenate([wg[:, 0], wg[:, 1]], axis=2).astype(BF16)
    return dict(
        ln_g=p["ln_g"].astype(F32).reshape(depth, 4, 1, d),
        ln_b=p["ln_b"].astype(F32).reshape(depth, 4, 1, d),
        w12=p["ffn_w12"].astype(BF16), w3=p["ffn_w3"].astype(BF16),
        w_in=w_packed, wgl=wgl,
        dcw=p["dn_conv_w"].astype(F32), lcw=p["lru_conv_w"].astype(F32),
        lcb=p["lru_conv_b"].astype(F32)[:, None, :],
        alog=lane_row(p["dn_a_log"]), dtb=lane_row(p["dn_dt_bias"]),
        nw=p["dn_norm_w"].astype(F32)[:, None, :],
        wg=wg, gb=p["lru_gate_b"].astype(F32).reshape(depth, 1, 2 * LRU_W),
        lam=p["lru_lambda"].astype(F32)[:, None, :],
        rel=_band_rel_rows(p["la_rel_bias"]),
        wb=p["w_branch"].astype(BF16), wout=p["mix_w_out"].astype(BF16),
        wq=p["xa_wq"].astype(BF16), wkv=p["xa_wkv"].astype(BF16), wo=p["xa_wo"].astype(BF16))


def _layer(x, mem_b, l, w):
    batch, seq, d = x.shape
    t = batch * seq

    x, xb = _ffn(x.reshape(t, d), w["w12"], w["w3"], w["ln_g"], w["ln_b"], (l, 0))

    qkvz, ba, xg, aqkv = _mixer_in(xb, w["w_in"], w["dcw"], w["lcw"], w["lcb"], l, seq=seq)
    qkvz = qkvz.reshape(batch, seq, -1)
    ya = _delta(qkvz, ba.reshape(batch, seq, LANES), w["alog"], w["dtb"], w["nw"], l)
    yc = _band_attention(aqkv.reshape(batch, seq, -1), w["rel"], l, batch=batch, seq=seq)
    x, xb = _mix_out(ya.reshape(t, -1), xg, yc.reshape(t, -1), xb, x, w["wg"], w["gb"], w["lam"],
                     w["wgl"], w["wb"], w["wout"], w["ln_g"], w["ln_b"], l, seq=seq)

    kv = _proj_plain(mem_b, w["wkv"], l, tn=D_MODEL, out_dtype=BF16, name="proj_kv")
    x, xb = _xattn(xb.reshape(batch, seq, d), x.reshape(batch, seq, d), kv.reshape(batch, MEM_LEN, 2 * d),
                   w["wq"], w["wo"], w["ln_g"], w["ln_b"], l)

    x, _ = _ffn(x.reshape(t, d), w["w12"], w["w3"], w["ln_g"], w["ln_b"], (l, 1))
    return x.reshape(batch, seq, d)


def kernel(x, mem, ln_g, ln_b, ffn_w12, ffn_w3, mix_w_in, dn_conv_w, dn_a_log, dn_dt_bias, dn_norm_w,
           lru_conv_w, lru_conv_b, lru_gate_w, lru_gate_b, lru_lambda, la_rel_bias, w_branch, mix_w_out,
           xa_wq, xa_wkv, xa_wo):
    w = _prepare(dict(
        ln_g=ln_g, ln_b=ln_b, ffn_w12=ffn_w12, ffn_w3=ffn_w3, mix_w_in=mix_w_in, dn_conv_w=dn_conv_w,
        dn_a_log=dn_a_log, dn_dt_bias=dn_dt_bias, dn_norm_w=dn_norm_w, lru_conv_w=lru_conv_w,
        lru_conv_b=lru_conv_b, lru_gate_w=lru_gate_w, lru_gate_b=lru_gate_b, lru_lambda=lru_lambda,
        la_rel_bias=la_rel_bias, w_branch=w_branch, mix_w_out=mix_w_out, xa_wq=xa_wq, xa_wkv=xa_wkv,
        xa_wo=xa_wo))
    batch = x.shape[0]
    mem_b = mem.reshape(batch * MEM_LEN, D_MODEL).astype(BF16)
    x = x.astype(F32)
    for l in range(DEPTH):
        x = _layer(x, mem_b, l, w)
    return x
```
